```python
import math
import jax, jax.numpy as jnp
from jax import lax
import numpy as np

D_MODEL = 2048
BATCH = 4
SEQ = 4096
DEPTH = 1
DEC_BATCH = 8
DEC_SEQ = 16
PAST_LEN = 4096

CHUNK = 64
N_HEADS = 8
HEAD_DIM = 128
ATTN_WIDTH = N_HEADS * HEAD_DIM
POOL_WINDOWS = (2, 4, 8, 16)
POOL_GROUPS = len(POOL_WINDOWS)
POOL_WIDTH = D_MODEL // 2
POOL_GROUP_WIDTH = POOL_WIDTH // POOL_GROUPS
POOL_HIST = max(POOL_WINDOWS) - 1
MIX_WIDTH = POOL_WIDTH + ATTN_WIDTH
IN_WIDTH = POOL_WIDTH + 3 * ATTN_WIDTH + 2 * D_MODEL
D_FF = int(math.ceil(8 * D_MODEL / 3 / 256) * 256)
Q_BLOCK = 128
EPS = 1e-6

kernel_name = "pool_stickbreak_gated_streaming_encoder"


def rmsnorm(x, g):
    xf = x.astype(jnp.float32)
    xf = xf * lax.rsqrt(jnp.mean(xf * xf, axis=-1, keepdims=True) + EPS)
    return (xf * g.astype(jnp.float32)).astype(x.dtype)


def multiscale_pool(u_hist, u, pos0, w_pool, s_pool):
    B, T, P = u.shape
    up = jnp.concatenate([u_hist.astype(u.dtype), u], axis=1)
    upf = up.astype(jnp.float32)
    cs = jnp.concatenate([jnp.zeros((B, 1, P), jnp.float32), jnp.cumsum(upf, axis=1)], axis=1)
    end = cs[:, POOL_HIST + 1:POOL_HIST + 1 + T]
    pos = pos0 + jnp.arange(T, dtype=jnp.int32)
    outs = []
    for gi, w in enumerate(POOL_WINDOWS):
        sl = slice(gi * POOL_GROUP_WIDTH, (gi + 1) * POOL_GROUP_WIDTH)
        start = cs[:, POOL_HIST + 1 - w:POOL_HIST + 1 - w + T, sl]
        cnt = jnp.minimum(pos + 1, w).astype(jnp.float32)[None, :, None]
        outs.append((end[..., sl] - start) / cnt)
    pooled = jnp.concatenate(outs, axis=-1)
    diff = (pooled - upf[:, POOL_HIST:]).reshape(B, T, POOL_GROUPS, POOL_GROUP_WIDTH)
    o = jnp.einsum('btgc,gce->btge', diff, w_pool.astype(jnp.float32)).reshape(B, T, P)
    o = o * s_pool.astype(jnp.float32)
    return o.astype(u.dtype), up[:, -POOL_HIST:]


def stick_breaking(q, k, v, q_pos, k_pos):
    z = jnp.einsum('bqhd,bkhd->bhqk', q, k).astype(jnp.float32) * (HEAD_DIM ** -0.5)
    mask = k_pos[None, :] < q_pos[:, None]
    log_1m = jnp.where(mask, jax.nn.log_sigmoid(-z), 0.0)
    suffix = lax.cumsum(log_1m, axis=3, reverse=True) - log_1m
    w = jnp.where(mask, jnp.exp(jax.nn.log_sigmoid(z) + suffix), 0.0)
    return jnp.einsum('bhqk,bkhd->bqhd', w.astype(v.dtype), v)


def layer(x, pool_hist, k_past, v_past, g_mix, w_in, w_pool, s_pool, w_branch, w_out,
          g_ffn, w_gate_up, w_down):
    B, T, _ = x.shape
    pos0 = 0 if k_past is None else k_past.shape[1]
    n = rmsnorm(x, g_mix)
    proj = n @ w_in
    c0 = POOL_WIDTH
    c1 = c0 + ATTN_WIDTH
    c2 = c1 + ATTN_WIDTH
    c3 = c2 + ATTN_WIDTH
    u = proj[..., :c0]
    q = proj[..., c0:c1].reshape(B, T, N_HEADS, HEAD_DIM)
    k = proj[..., c1:c2].reshape(B, T, N_HEADS, HEAD_DIM)
    v = proj[..., c2:c3].reshape(B, T, N_HEADS, HEAD_DIM)
    gate_logits = proj[..., c3:]

    o_a, new_pool = multiscale_pool(pool_hist, u, pos0, w_pool, s_pool)

    if k_past is None:
        k_all, v_all = k, v
    else:
        k_all = jnp.concatenate([k_past.astype(k.dtype), k], axis=1)
        v_all = jnp.concatenate([v_past.astype(v.dtype), v], axis=1)
    k_pos = jnp.arange(k_all.shape[1], dtype=jnp.int32)
    q_pos = pos0 + jnp.arange(T, dtype=jnp.int32)
    if T > Q_BLOCK:
        nb = T // Q_BLOCK
        qb = q.reshape(B, nb, Q_BLOCK, N_HEADS, HEAD_DIM).transpose(1, 0, 2, 3, 4)
        pb = q_pos.reshape(nb, Q_BLOCK)
        ob = lax.map(lambda a: stick_breaking(a[0], k_all, v_all, a[1], k_pos), (qb, pb))
        o_b = ob.transpose(1, 0, 2, 3, 4).reshape(B, T, ATTN_WIDTH)
    else:
        o_b = stick_breaking(q, k_all, v_all, q_pos, k_pos).reshape(B, T, ATTN_WIDTH)

    y_a = o_a @ w_branch[:POOL_WIDTH]
    y_b = o_b @ w_branch[POOL_WIDTH:]
    g = jax.nn.sigmoid(gate_logits.astype(jnp.float32))
    merged = g[..., :D_MODEL] * y_a.astype(jnp.float32) + g[..., D_MODEL:] * y_b.astype(jnp.float32)
    h = x + merged.astype(x.dtype) @ w_out

    n2 = rmsnorm(h, g_ffn)
    gu = n2 @ w_gate_up
    hid = jax.nn.silu(gu[..., :D_FF]) * gu[..., D_FF:]
    h = h + hid @ w_down
    return h, k, v, new_pool


def setup_inputs(seed: int = 0) -> dict:
    key = jax.random.key(seed)
    ks = jax.random.split(key, 16)
    f32 = jnp.float32
    nrm = lambda k, s, sc: jax.random.normal(k, s, f32) * sc
    return {
        "x_prompt": nrm(ks[0], (BATCH, SEQ, D_MODEL), 1.0),
        "x_sample": nrm(ks[1], (DEC_BATCH, DEC_SEQ, D_MODEL), 1.0),
        "cache_k": nrm(ks[2], (DEPTH, DEC_BATCH, PAST_LEN, N_HEADS, HEAD_DIM), 1.0),
        "cache_v": nrm(ks[3], (DEPTH, DEC_BATCH, PAST_LEN, N_HEADS, HEAD_DIM), 1.0),
        "state_pool": nrm(ks[4], (DEPTH, DEC_BATCH, POOL_HIST, POOL_WIDTH), 1.0),
        "g_mix": 1.0 + nrm(ks[5], (DEPTH, D_MODEL), 0.02),
        "w_in": nrm(ks[6], (DEPTH, D_MODEL, IN_WIDTH), D_MODEL ** -0.5),
        "w_pool": nrm(ks[7], (DEPTH, POOL_GROUPS, POOL_GROUP_WIDTH, POOL_GROUP_WIDTH), POOL_GROUP_WIDTH ** -0.5),
        "s_pool": 1.0 + nrm(ks[8], (DEPTH, POOL_WIDTH), 0.02),
        "w_branch": nrm(ks[9], (DEPTH, MIX_WIDTH, D_MODEL), POOL_WIDTH ** -0.5),
        "w_out": nrm(ks[10], (DEPTH, D_MODEL, D_MODEL), D_MODEL ** -0.5),
        "g_ffn": 1.0 + nrm(ks[11], (DEPTH, D_MODEL), 0.02),
        "w_gate_up": nrm(ks[12], (DEPTH, D_MODEL, 2 * D_FF), D_MODEL ** -0.5),
        "w_down": nrm(ks[13], (DEPTH, D_FF, D_MODEL), D_FF ** -0.5),
        "g_final": 1.0 + nrm(ks[14], (D_MODEL,), 0.02),
    }


def reference(x_prompt, x_sample, cache_k, cache_v, state_pool, g_mix, w_in, w_pool, s_pool,
              w_branch, w_out, g_ffn, w_gate_up, w_down, g_final):
    hp = x_prompt
    hs = x_sample
    kp_l, vp_l, pp_l, ks_l, vs_l, ps_l = [], [], [], [], [], []
    for l in range(DEPTH):
        params = (g_mix[l], w_in[l], w_pool[l], s_pool[l], w_branch[l], w_out[l],
                  g_ffn[l], w_gate_up[l], w_down[l])
        zero_hist = jnp.zeros((hp.shape[0], POOL_HIST, POOL_WIDTH), hp.dtype)
        hp, kp, vp, pp = layer(hp, zero_hist, None, None, *params)
        hs, ksm, vsm, psm = layer(hs, state_pool[l], cache_k[l], cache_v[l], *params)
        kp_l.append(kp); vp_l.append(vp); pp_l.append(pp)
        ks_l.append(ksm); vs_l.append(vsm); ps_l.append(psm)
    y_prompt = rmsnorm(hp, g_final)
    y_sample = rmsnorm(hs, g_final)
    return (y_prompt, y_sample, jnp.stack(kp_l), jnp.stack(vp_l), jnp.stack(pp_l),
            jnp.stack(ks_l), jnp.stack(vs_l), jnp.stack(ps_l))
```

```python
import functools

import jax
import jax.numpy as jnp
from jax import lax
from jax.experimental import pallas as pl
from jax.experimental.pallas import tpu as pltpu

F32 = jnp.float32
BF16 = jnp.bfloat16

D_MODEL = 2048
N_HEADS = 8
HEAD_DIM = 128
ATTN_WIDTH = N_HEADS * HEAD_DIM
POOL_WINDOWS = (2, 4, 8, 16)
POOL_WIDTH = D_MODEL // 2
POOL_GROUP_WIDTH = POOL_WIDTH // len(POOL_WINDOWS)
POOL_HIST = max(POOL_WINDOWS) - 1
HIST_ROWS = POOL_HIST + 1
GROUP_WIDTH = 1024
N_GATE_GROUPS = 2 * D_MODEL // GROUP_WIDTH
D_FF = 5632
EPS = 1e-6

LANES = 128
KEY_TILE = 2 * LANES
Q_TILE = 256
VMEM_LIMIT = 56 * 1024 * 1024


def _params(*sem):
    return pltpu.CompilerParams(dimension_semantics=sem, vmem_limit_bytes=VMEM_LIMIT)


def _rmsnorm(x, g):
    ms = jnp.mean(x * x, axis=-1, keepdims=True)
    return x * lax.rsqrt(ms + EPS) * g


def _sigmoid(x):
    return 1.0 / (1.0 + jnp.exp(-x))


def _inproj_kernel(x_ref, g_ref, w_ref, u_ref, q_ref, k_ref, v_ref, gate_ref, xn_ref, *, tpg):
    j = pl.program_id(1)

    @pl.when(j == 0)
    def _():
        xn_ref[...] = _rmsnorm(x_ref[...], g_ref[...]).astype(BF16)

    def proj():
        return jnp.dot(xn_ref[...], w_ref[...], preferred_element_type=F32)

    @pl.when(j < tpg)
    def _():
        u_ref[...] = proj()

    @pl.when((j >= tpg) & (j < 2 * tpg))
    def _():
        q_ref[...] = proj() * (HEAD_DIM ** -0.5)

    @pl.when((j >= 2 * tpg) & (j < 3 * tpg))
    def _():
        k_ref[...] = proj()

    @pl.when((j >= 3 * tpg) & (j < 4 * tpg))
    def _():
        v_ref[...] = proj()

    @pl.when(j >= 4 * tpg)
    def _():
        gate_ref[...] = proj()


def _in_proj(x, g_mix, w_in, *, tm, tn):
    n = x.shape[0]
    tpg = GROUP_WIDTH // tn
    n_tiles = w_in.shape[1] // tn

    def group_spec(g, ntiles):
        return pl.BlockSpec((tm, tn), lambda i, j: (i, jnp.clip(j - g * tpg, 0, ntiles - 1)))

    small = jax.ShapeDtypeStruct((n, GROUP_WIDTH), F32)
    return pl.pallas_call(
        functools.partial(_inproj_kernel, tpg=tpg),
        grid=(n // tm, n_tiles),
        in_specs=[
            pl.BlockSpec((tm, D_MODEL), lambda i, j: (i, 0)),
            pl.BlockSpec((1, D_MODEL), lambda i, j: (0, 0)),
            pl.BlockSpec((D_MODEL, tn), lambda i, j: (0, j)),
        ],
        out_specs=[group_spec(0, tpg), group_spec(1, tpg), group_spec(2, tpg), group_spec(3, tpg),
                   group_spec(4, N_GATE_GROUPS * tpg)],
        out_shape=[small, small, small, small,
                   jax.ShapeDtypeStruct((n, N_GATE_GROUPS * GROUP_WIDTH), F32)],
        scratch_shapes=[pltpu.VMEM((tm, D_MODEL), BF16)],
        compiler_params=_params("arbitrary", "arbitrary"),
        name="in_proj",
    )(x, g_mix, w_in)


def _pool_kernel(u_ref, uprev_ref, hist_ref, wp_ref, sp_ref, o_ref, buf_ref, *, tm, pos0):
    i = pl.program_id(1)
    buf_ref[HIST_ROWS:, :] = u_ref[...]

    @pl.when(i == 0)
    def _():
        buf_ref[:HIST_ROWS, :] = hist_ref[0]

    @pl.when(i > 0)
    def _():
        buf_ref[:HIST_ROWS, :] = uprev_ref[...]

    pos = pos0 + i * tm + lax.broadcasted_iota(jnp.int32, (tm, 1), 0)
    for g, w in enumerate(POOL_WINDOWS):
        cols = slice(g * POOL_GROUP_WIDTH, (g + 1) * POOL_GROUP_WIDTH)
        cur = buf_ref[HIST_ROWS:, cols]
        s = cur
        for d in range(1, w):
            s = s + buf_ref[HIST_ROWS - d:HIST_ROWS - d + tm, cols]
        cnt = jnp.minimum(pos + 1, w).astype(F32)
        diff = s / cnt - cur
        o = jnp.dot(diff.astype(BF16), wp_ref[g], preferred_element_type=F32)
        o_ref[:, cols] = (o * sp_ref[:, cols]).astype(BF16)


def _pool(u, hist, w_pool, s_pool, *, batch, seq, tm, pos0):
    nt = seq // tm
    per = tm // HIST_ROWS
    return pl.pallas_call(
        functools.partial(_pool_kernel, tm=tm, pos0=pos0),
        grid=(batch, nt),
        in_specs=[
            pl.BlockSpec((tm, POOL_WIDTH), lambda b, i: (b * nt + i, 0)),
            pl.BlockSpec((HIST_ROWS, POOL_WIDTH),
                         lambda b, i: (jnp.maximum((b * nt + i) * per - 1, 0), 0)),
            pl.BlockSpec((1, HIST_ROWS, POOL_WIDTH), lambda b, i: (b, 0, 0)),
            pl.BlockSpec(w_pool.shape, lambda b, i: (0, 0, 0)),
            pl.BlockSpec((1, POOL_WIDTH), lambda b, i: (0, 0)),
        ],
        out_specs=pl.BlockSpec((tm, POOL_WIDTH), lambda b, i: (b * nt + i, 0)),
        out_shape=jax.ShapeDtypeStruct((batch * seq, POOL_WIDTH), BF16),
        scratch_shapes=[pltpu.VMEM((HIST_ROWS + tm, POOL_WIDTH), F32)],
        compiler_params=_params("arbitrary", "arbitrary"),
        name="pool",
    )(u, u, hist, w_pool, s_pool)


def _suffix_matrix():
    r = lax.broadcasted_iota(jnp.int32, (2 * LANES, 2 * LANES), 0) % LANES
    c = lax.broadcasted_iota(jnp.int32, (2 * LANES, 2 * LANES), 1)
    return jnp.where((c >= LANES) | (r > c), 1.0, 0.0).astype(BF16)


def _sb_tile(q, kblk, vblk, carry, acc, sfx, mask):
    z = lax.dot_general(q, kblk, (((1,), (1,)), ((), ())), preferred_element_type=F32)
    t = jnp.log(1.0 + jnp.exp(-jnp.abs(z)))
    sp = jnp.maximum(z, 0.0) + t
    lsz = jnp.minimum(z, 0.0) - t
    if mask is not None:
        sp = jnp.where(mask, sp, 0.0)
    hi = sp.astype(BF16)
    lo = (sp - hi.astype(F32)).astype(BF16)
    ws = [None, None]
    for half in (1, 0):
        sl = slice(half * LANES, (half + 1) * LANES)
        lhs = jnp.concatenate([hi[:, sl], lo[:, sl]], axis=1)
        r = jnp.dot(lhs, sfx, preferred_element_type=F32)
        w = jnp.exp(lsz[:, sl] - (r[:, :LANES] + carry))
        if mask is not None:
            w = jnp.where(mask[:, sl], w, 0.0)
        ws[half] = w.astype(BF16)
        carry = carry + r[:, LANES:]
    w = jnp.concatenate(ws, axis=1)
    acc = acc + jnp.dot(w, vblk, preferred_element_type=F32)
    return carry, acc


def _causal_mask(rows):
    r = lax.broadcasted_iota(jnp.int32, (rows, KEY_TILE), 0)
    c = lax.broadcasted_iota(jnp.int32, (rows, KEY_TILE), 1)
    return c < r


def _attn_prompt_kernel(q_ref, k_ref, v_ref, o_ref, qb_ref, kb_ref, vb_ref, sfx_ref, *, seq):
    qb_ref[...] = q_ref[...].astype(BF16)
    kb_ref[...] = k_ref[...].astype(BF16)
    vb_ref[...] = v_ref[...].astype(BF16)
    sfx_ref[...] = _suffix_matrix()

    def q_block(qi, c):
        q0 = pl.multiple_of(qi * Q_TILE, Q_TILE)
        zeros = jnp.zeros((Q_TILE, LANES), F32)
        state = _sb_tile(qb_ref[pl.ds(q0, Q_TILE), :], kb_ref[pl.ds(q0, KEY_TILE), :],
                         vb_ref[pl.ds(q0, KEY_TILE), :], zeros, zeros, sfx_ref[...],
                         _causal_mask(Q_TILE))

        def body(it, st):
            k0 = pl.multiple_of((qi - 1 - it) * KEY_TILE, KEY_TILE)
            return _sb_tile(qb_ref[pl.ds(q0, Q_TILE), :], kb_ref[pl.ds(k0, KEY_TILE), :],
                            vb_ref[pl.ds(k0, KEY_TILE), :], st[0], st[1], sfx_ref[...], None)

        _, acc = lax.fori_loop(0, qi, body, state)
        o_ref[pl.ds(q0, Q_TILE), :] = acc.astype(BF16)
        return c

    lax.fori_loop(0, seq // Q_TILE, q_block, 0)


def _attn_prompt(q, k, v, *, batch, seq):
    assert Q_TILE == KEY_TILE and seq % Q_TILE == 0
    spec = pl.BlockSpec((seq, HEAD_DIM), lambda b, h: (b, h))
    return pl.pallas_call(
        functools.partial(_attn_prompt_kernel, seq=seq),
        grid=(batch, N_HEADS),
        in_specs=[spec, spec, spec],
        out_specs=spec,
        out_shape=jax.ShapeDtypeStruct((batch * seq, ATTN_WIDTH), BF16),
        scratch_shapes=[pltpu.VMEM((seq, HEAD_DIM), BF16)] * 3
        + [pltpu.VMEM((2 * LANES, 2 * LANES), BF16)],
        compiler_params=_params("arbitrary", "arbitrary"),
        name="attn_prompt",
    )(q, k, v)


def _attn_decode_kernel(q_ref, kn_ref, vn_ref, kp_ref, vp_ref, o_ref, kb_ref, vb_ref, sfx_ref,
                        *, past, rows):
    sfx_ref[...] = _suffix_matrix()
    for src_new, src_past, dst in ((kn_ref, kp_ref, kb_ref), (vn_ref, vp_ref, vb_ref)):
        dst[:past, :] = src_past[...].astype(BF16)
        dst[past:, :] = jnp.zeros((KEY_TILE, HEAD_DIM), BF16)
        dst[past:past + rows, :] = src_new[...].astype(BF16)
    q = q_ref[...].astype(BF16)
    zeros = jnp.zeros((rows, LANES), F32)
    state = _sb_tile(q, kb_ref[past:, :], vb_ref[past:, :], zeros, zeros, sfx_ref[...],
                     _causal_mask(rows))

    def body(it, st):
        k0 = pl.multiple_of(past - (it + 1) * KEY_TILE, KEY_TILE)
        return _sb_tile(q, kb_ref[pl.ds(k0, KEY_TILE), :], vb_ref[pl.ds(k0, KEY_TILE), :],
                        st[0], st[1], sfx_ref[...], None)

    _, acc = lax.fori_loop(0, past // KEY_TILE, body, state)
    o_ref[...] = acc.astype(BF16)


def _attn_decode(q, k_new, v_new, k_past, v_past, *, batch, rows, past):
    assert rows <= KEY_TILE and past % KEY_TILE == 0
    new_spec = pl.BlockSpec((rows, HEAD_DIM), lambda b, h: (b, h))
    past_spec = pl.BlockSpec((past, HEAD_DIM), lambda b, h: (b, h))
    return pl.pallas_call(
        functools.partial(_attn_decode_kernel, past=past, rows=rows),
        grid=(batch, N_HEADS),
        in_specs=[new_spec, new_spec, new_spec, past_spec, past_spec],
        out_specs=new_spec,
        out_shape=jax.ShapeDtypeStruct((batch * rows, ATTN_WIDTH), BF16),
        scratch_shapes=[pltpu.VMEM((past + KEY_TILE, HEAD_DIM), BF16)] * 2
        + [pltpu.VMEM((2 * LANES, 2 * LANES), BF16)],
        compiler_params=_params("arbitrary", "arbitrary"),
        name="attn_decode",
    )(q, k_new, v_new, k_past, v_past)


def _merge_kernel(oa_ref, ob_ref, wa_ref, wb_ref, ga_ref, gb_ref, out_ref):
    ya = jnp.dot(oa_ref[...], wa_ref[...], preferred_element_type=F32)
    yb = jnp.dot(ob_ref[...], wb_ref[...], preferred_element_type=F32)
    out_ref[...] = (_sigmoid(ga_ref[...]) * ya + _sigmoid(gb_ref[...]) * yb).astype(BF16)


def _merge(o_a, o_b, w_a, w_b, gates, *, tm, tn):
    n = o_a.shape[0]
    nj = D_MODEL // tn
    return pl.pallas_call(
        _merge_kernel,
        grid=(n // tm, nj),
        in_specs=[
            pl.BlockSpec((tm, POOL_WIDTH), lambda i, j: (i, 0)),
            pl.BlockSpec((tm, ATTN_WIDTH), lambda i, j: (i, 0)),
            pl.BlockSpec((POOL_WIDTH, tn), lambda i, j: (0, j)),
            pl.BlockSpec((ATTN_WIDTH, tn), lambda i, j: (0, j)),
            pl.BlockSpec((tm, tn), lambda i, j: (i, j)),
            pl.BlockSpec((tm, tn), lambda i, j: (i, nj + j)),
        ],
        out_specs=pl.BlockSpec((tm, tn), lambda i, j: (i, j)),
        out_shape=jax.ShapeDtypeStruct((n, D_MODEL), BF16),
        compiler_params=_params("arbitrary", "arbitrary"),
        name="merge",
    )(o_a, o_b, w_a, w_b, gates, gates)


def _outproj_kernel(m_ref, w_ref, x_ref, h_ref):
    h_ref[...] = x_ref[...] + jnp.dot(m_ref[...], w_ref[...], preferred_element_type=F32)


def _out_proj(merged, w_out, x, *, tm, tn):
    n = x.shape[0]
    return pl.pallas_call(
        _outproj_kernel,
        grid=(n // tm, D_MODEL // tn),
        in_specs=[
            pl.BlockSpec((tm, D_MODEL), lambda i, j: (i, 0)),
            pl.BlockSpec((D_MODEL, tn), lambda i, j: (0, j)),
            pl.BlockSpec((tm, tn), lambda i, j: (i, j)),
        ],
        out_specs=pl.BlockSpec((tm, tn), lambda i, j: (i, j)),
        out_shape=jax.ShapeDtypeStruct((n, D_MODEL), F32),
        compiler_params=_params("arbitrary", "arbitrary"),
        name="out_proj",
    )(merged, w_out, x)


def _ffn_up_kernel(h_ref, g_ref, wg_ref, wu_ref, hid_ref, n_ref):
    @pl.when(pl.program_id(1) == 0)
    def _():
        n_ref[...] = _rmsnorm(h_ref[...], g_ref[...]).astype(BF16)

    gate = jnp.dot(n_ref[...], wg_ref[...], preferred_element_type=F32)
    up = jnp.dot(n_ref[...], wu_ref[...], preferred_element_type=F32)
    hid_ref[...] = (gate * _sigmoid(gate) * up).astype(BF16)


def _ffn_up(h, g_ffn, w_gate_up, *, tm, tn):
    n = h.shape[0]
    nj = D_FF // tn
    return pl.pallas_call(
        _ffn_up_kernel,
        grid=(n // tm, nj),
        in_specs=[
            pl.BlockSpec((tm, D_MODEL), lambda i, j: (i, 0)),
            pl.BlockSpec((1, D_MODEL), lambda i, j: (0, 0)),
            pl.BlockSpec((D_MODEL, tn), lambda i, j: (0, j)),
            pl.BlockSpec((D_MODEL, tn), lambda i, j: (0, nj + j)),
        ],
        out_specs=pl.BlockSpec((tm, tn), lambda i, j: (i, j)),
        out_shape=jax.ShapeDtypeStruct((n, D_FF), BF16),
        scratch_shapes=[pltpu.VMEM((tm, D_MODEL), BF16)],
        compiler_params=_params("arbitrary", "arbitrary"),
        name="ffn_up",
    )(h, g_ffn, w_gate_up, w_gate_up)


def _ffn_down_kernel(hid_ref, w_ref, h_ref, g_ref, y_ref):
    k = pl.program_id(1)
    part = jnp.dot(hid_ref[...], w_ref[...], preferred_element_type=F32)

    @pl.when(k == 0)
    def _():
        y_ref[...] = h_ref[...] + part

    @pl.when(k > 0)
    def _():
        y_ref[...] += part

    @pl.when(k == pl.num_programs(1) - 1)
    def _():
        y_ref[...] = _rmsnorm(y_ref[...], g_ref[...])


def _ffn_down(hid, w_down, h, g_final, *, tm, tk):
    n = h.shape[0]
    return pl.pallas_call(
        _ffn_down_kernel,
        grid=(n // tm, D_FF // tk),
        in_specs=[
            pl.BlockSpec((tm, tk), lambda i, k: (i, k)),
            pl.BlockSpec((tk, D_MODEL), lambda i, k: (k, 0)),
            pl.BlockSpec((tm, D_MODEL), lambda i, k: (i, 0)),
            pl.BlockSpec((1, D_MODEL), lambda i, k: (0, 0)),
        ],
        out_specs=pl.BlockSpec((tm, D_MODEL), lambda i, k: (i, 0)),
        out_shape=jax.ShapeDtypeStruct((n, D_MODEL), F32),
        compiler_params=_params("arbitrary", "arbitrary"),
        name="ffn_down",
    )(hid, w_down, h, g_final)


def _layer(x, hist, past_kv, weights, g_final, *, batch, seq, pos0, tm, pool_tm):
    g_mix, w_in, w_pool, s_pool, w_a, w_b, w_out, g_ffn, w_gate_up, w_down = weights
    u, q, k, v, gates = _in_proj(x, g_mix, w_in, tm=tm, tn=512)
    o_a = _pool(u, hist, w_pool, s_pool, batch=batch, seq=seq, tm=pool_tm, pos0=pos0)
    if past_kv is None:
        o_b = _attn_prompt(q, k, v, batch=batch, seq=seq)
    else:
        o_b = _attn_decode(q, k, v, past_kv[0], past_kv[1], batch=batch, rows=seq, past=pos0)
    merged = _merge(o_a, o_b, w_a, w_b, gates, tm=tm, tn=512)
    h = _out_proj(merged, w_out, x, tm=tm, tn=512)
    hid = _ffn_up(h, g_ffn, w_gate_up, tm=tm, tn=512)
    y = _ffn_down(hid, w_down, h, g_final, tm=tm, tk=512)
    return y, u, k, v


def kernel(x_prompt, x_sample, cache_k, cache_v, state_pool, g_mix, w_in, w_pool, s_pool,
           w_branch, w_out, g_ffn, w_gate_up, w_down, g_final):
    depth = w_in.shape[0]
    assert depth == 1
    b_p, t_p, _ = x_prompt.shape
    b_s, t_s, _ = x_sample.shape
    past = cache_k.shape[2]
    l = 0
    weights = (
        g_mix[l][None, :], w_in[l].astype(BF16), w_pool[l].astype(BF16), s_pool[l][None, :],
        w_branch[l, :POOL_WIDTH].astype(BF16), w_branch[l, POOL_WIDTH:].astype(BF16),
        w_out[l].astype(BF16), g_ffn[l][None, :], w_gate_up[l].astype(BF16),
        w_down[l].astype(BF16),
    )
    g_fin = g_final[None, :]

    hist_p = jnp.zeros((b_p, HIST_ROWS, POOL_WIDTH), F32)
    y_p, u_p, k_p, v_p = _layer(
        x_prompt.reshape(b_p * t_p, D_MODEL), hist_p, None, weights, g_fin,
        batch=b_p, seq=t_p, pos0=0, tm=1024, pool_tm=512)

    hist_s = jnp.pad(state_pool[l], ((0, 0), (HIST_ROWS - POOL_HIST, 0), (0, 0)))
    past_kv = (cache_k[l].reshape(b_s * past, ATTN_WIDTH), cache_v[l].reshape(b_s * past, ATTN_WIDTH))
    y_s, u_s, k_s, v_s = _layer(
        x_sample.reshape(b_s * t_s, D_MODEL), hist_s, past_kv, weights, g_fin,
        batch=b_s, seq=t_s, pos0=past, tm=b_s * t_s, pool_tm=t_s)

    new_pool_p = u_p.reshape(b_p, t_p, POOL_WIDTH)[:, t_p - POOL_HIST:]
    new_pool_s = jnp.concatenate([state_pool[l], u_s.reshape(b_s, t_s, POOL_WIDTH)],
                                 axis=1)[:, -POOL_HIST:]
    return (
        y_p.reshape(b_p, t_p, D_MODEL),
        y_s.reshape(b_s, t_s, D_MODEL),
        k_p.reshape(1, b_p, t_p, N_HEADS, HEAD_DIM),
        v_p.reshape(1, b_p, t_p, N_HEADS, HEAD_DIM),
        new_pool_p[None],
        k_s.reshape(1, b_s, t_s, N_HEADS, HEAD_DIM),
        v_s.reshape(1, b_s, t_s, N_HEADS, HEAD_DIM),
        new_pool_s[None],
    )
```

```python
import functools

import jax
import jax.numpy as jnp
from jax import lax
from jax.experimental import pallas as pl
from jax.experimental.pallas import tpu as pltpu

F32 = jnp.float32
BF16 = jnp.bfloat16

D_MODEL = 2048
N_HEADS = 8
HEAD_DIM = 128
ATTN_WIDTH = N_HEADS * HEAD_DIM
POOL_WINDOWS = (2, 4, 8, 16)
POOL_WIDTH = D_MODEL // 2
POOL_GROUP_WIDTH = POOL_WIDTH // len(POOL_WINDOWS)
POOL_HIST = max(POOL_WINDOWS) - 1
HIST_ROWS = POOL_HIST + 1
GROUP_WIDTH = 1024
N_GATE_GROUPS = 2 * D_MODEL // GROUP_WIDTH
D_FF = 5632
EPS = 1e-6

LANES = 128
DECODE_KEYS = 1024
Q_TILE = 512
Q_SCALE = 1.4426950408889634 * HEAD_DIM ** -0.5
VMEM_LIMIT = 56 * 1024 * 1024


def _params(*sem):
    return pltpu.CompilerParams(dimension_semantics=sem, vmem_limit_bytes=VMEM_LIMIT)


def _rmsnorm(x, g):
    ms = jnp.mean(x * x, axis=-1, keepdims=True)
    return x * lax.rsqrt(ms + EPS) * g


def _sigmoid(x):
    return 1.0 / (1.0 + jnp.exp(-x))


def _inproj_kernel(x_ref, g_ref, w_ref, u_ref, q_ref, k_ref, v_ref, gate_ref, xn_ref, *, tpg):
    j = pl.program_id(1)

    @pl.when(j == 0)
    def _():
        xn_ref[...] = _rmsnorm(x_ref[...], g_ref[...]).astype(BF16)

    def proj():
        return jnp.dot(xn_ref[...], w_ref[...], preferred_element_type=F32)

    @pl.when(j < tpg)
    def _():
        u_ref[...] = proj()

    @pl.when((j >= tpg) & (j < 2 * tpg))
    def _():
        q_ref[...] = proj() * Q_SCALE

    @pl.when((j >= 2 * tpg) & (j < 3 * tpg))
    def _():
        k_ref[...] = proj()

    @pl.when((j >= 3 * tpg) & (j < 4 * tpg))
    def _():
        v_ref[...] = proj()

    @pl.when(j >= 4 * tpg)
    def _():
        gate_ref[...] = proj()


def _in_proj(x, g_mix, w_in, *, tm, tn):
    n = x.shape[0]
    tpg = GROUP_WIDTH // tn
    n_tiles = w_in.shape[1] // tn

    def group_spec(g, ntiles):
        return pl.BlockSpec((tm, tn), lambda i, j: (i, jnp.clip(j - g * tpg, 0, ntiles - 1)))

    small = jax.ShapeDtypeStruct((n, GROUP_WIDTH), F32)
    return pl.pallas_call(
        functools.partial(_inproj_kernel, tpg=tpg),
        grid=(n // tm, n_tiles),
        in_specs=[
            pl.BlockSpec((tm, D_MODEL), lambda i, j: (i, 0)),
            pl.BlockSpec((1, D_MODEL), lambda i, j: (0, 0)),
            pl.BlockSpec((D_MODEL, tn), lambda i, j: (0, j)),
        ],
        out_specs=[group_spec(0, tpg), group_spec(1, tpg), group_spec(2, tpg), group_spec(3, tpg),
                   group_spec(4, N_GATE_GROUPS * tpg)],
        out_shape=[small, small, small, small,
                   jax.ShapeDtypeStruct((n, N_GATE_GROUPS * GROUP_WIDTH), F32)],
        scratch_shapes=[pltpu.VMEM((tm, D_MODEL), BF16)],
        compiler_params=_params("arbitrary", "arbitrary"),
        name="in_proj",
    )(x, g_mix, w_in)


def _pool_kernel(u_ref, uprev_ref, hist_ref, wp_ref, sp_ref, o_ref, buf_ref, *, tm, pos0):
    i = pl.program_id(1)
    buf_ref[HIST_ROWS:, :] = u_ref[...]

    @pl.when(i == 0)
    def _():
        buf_ref[:HIST_ROWS, :] = hist_ref[0]

    @pl.when(i > 0)
    def _():
        buf_ref[:HIST_ROWS, :] = uprev_ref[...]

    pos = pos0 + i * tm + lax.broadcasted_iota(jnp.int32, (tm, 1), 0)
    for g, w in enumerate(POOL_WINDOWS):
        cols = slice(g * POOL_GROUP_WIDTH, (g + 1) * POOL_GROUP_WIDTH)
        cur = buf_ref[HIST_ROWS:, cols]
        s = cur
        for d in range(1, w):
            s = s + buf_ref[HIST_ROWS - d:HIST_ROWS - d + tm, cols]
        cnt = jnp.minimum(pos + 1, w).astype(F32)
        diff = s / cnt - cur
        o = jnp.dot(diff.astype(BF16), wp_ref[g], preferred_element_type=F32)
        o_ref[:, cols] = (o * sp_ref[:, cols]).astype(BF16)


def _pool(u, hist, w_pool, s_pool, *, batch, seq, tm, pos0):
    nt = seq // tm
    per = tm // HIST_ROWS
    return pl.pallas_call(
        functools.partial(_pool_kernel, tm=tm, pos0=pos0),
        grid=(batch, nt),
        in_specs=[
            pl.BlockSpec((tm, POOL_WIDTH), lambda b, i: (b * nt + i, 0)),
            pl.BlockSpec((HIST_ROWS, POOL_WIDTH),
                         lambda b, i: (jnp.maximum((b * nt + i) * per - 1, 0), 0)),
            pl.BlockSpec((1, HIST_ROWS, POOL_WIDTH), lambda b, i: (b, 0, 0)),
            pl.BlockSpec(w_pool.shape, lambda b, i: (0, 0, 0)),
            pl.BlockSpec((1, POOL_WIDTH), lambda b, i: (0, 0)),
        ],
        out_specs=pl.BlockSpec((tm, POOL_WIDTH), lambda b, i: (b * nt + i, 0)),
        out_shape=jax.ShapeDtypeStruct((batch * seq, POOL_WIDTH), BF16),
        scratch_shapes=[pltpu.VMEM((HIST_ROWS + tm, POOL_WIDTH), F32)],
        compiler_params=_params("arbitrary", "arbitrary"),
        name="pool",
    )(u, u, hist, w_pool, s_pool)


def _suffix_matrix():
    r = lax.broadcasted_iota(jnp.int32, (2 * LANES, 2 * LANES), 0) % LANES
    c = lax.broadcasted_iota(jnp.int32, (2 * LANES, 2 * LANES), 1)
    return jnp.where((c >= LANES) | (r >= c), 1.0, 0.0).astype(BF16)


def _scores(q, kblk):
    return lax.dot_general(q, kblk, (((1,), (1,)), ((), ())), preferred_element_type=F32)


def _sb_weights(z, carry, sfx, *, stack_chunks):
    rows, span = z.shape
    n_chunks = span // LANES
    neg_abs = pltpu.bitcast(pltpu.bitcast(z, jnp.int32) | jnp.int32(-2 ** 31), F32)
    sp = jnp.maximum(z, 0.0) + jnp.log2(1.0 + jnp.exp2(neg_abs))
    hi_f = pltpu.bitcast(pltpu.bitcast(sp, jnp.int32) & jnp.int32(-65536), F32)
    hi = hi_f.astype(BF16)
    lo = (sp - hi_f).astype(BF16)

    def chunk_lhs(c):
        sl = slice(c * LANES, (c + 1) * LANES)
        return jnp.concatenate([hi[:, sl], lo[:, sl]], axis=1)

    if stack_chunks:
        r_all = jnp.dot(jnp.concatenate([chunk_lhs(c) for c in range(n_chunks)], axis=0), sfx,
                        preferred_element_type=F32)
    ws = []
    for c in reversed(range(n_chunks)):
        if stack_chunks:
            r = r_all[c * rows:(c + 1) * rows]
        else:
            r = jnp.dot(chunk_lhs(c), sfx, preferred_element_type=F32)
        w = jnp.exp2(z[:, c * LANES:(c + 1) * LANES] - (r[:, :LANES] + carry))
        ws.insert(0, w.astype(BF16))
        carry = carry + r[:, LANES:]
    return carry, jnp.concatenate(ws, axis=1)


def _causal_bias(rows, span):
    r = lax.broadcasted_iota(jnp.int32, (rows, span), 0)
    c = lax.broadcasted_iota(jnp.int32, (rows, span), 1)
    return jnp.where(c < r, 0.0, MASKED_LOGIT)


ATTN_UNROLL = 2
MASKED_LOGIT = -1e9


def _attn_prompt_kernel(q_ref, k_ref, v_ref, o_ref, qb_ref, kb_ref, vt_ref, sfx_ref, bias_ref,
                        z_ref, w_ref, carry_ref, acct_ref, *, seq):
    tile = Q_TILE
    nq = seq // tile
    n_tiles = nq * (nq + 1) // 2
    assert ATTN_UNROLL % 2 == 0 and n_tiles % ATTN_UNROLL == 0
    qb_ref[...] = q_ref[...].astype(BF16)
    kb_ref[...] = k_ref[...].astype(BF16)
    for j in range(nq):
        vt_ref[j] = v_ref[j * tile:(j + 1) * tile, :].T.astype(BF16)

    @pl.when((pl.program_id(0) == 0) & (pl.program_id(1) == 0))
    def _():
        sfx_ref[...] = _suffix_matrix()
        bias_ref[0] = jnp.zeros((tile, tile), F32)
        bias_ref[1] = _causal_bias(tile, tile)

    carry_ref[...] = jnp.zeros(carry_ref.shape, F32)
    acct_ref[...] = jnp.zeros(acct_ref.shape, F32)
    w_ref[1] = jnp.zeros((tile, tile), BF16)

    def rows(ref, i):
        return ref[pl.ds(pl.multiple_of(i * tile, tile), tile), :]

    def following(t):
        qi, j = t
        wrap = j == 0
        return jnp.where(wrap, qi + 1, qi), jnp.where(wrap, qi + 1, j - 1)

    def logits(t, slot):
        qi, j = jnp.minimum(t[0], nq - 1), jnp.minimum(t[1], nq - 1)
        z_ref[slot] = _scores(rows(qb_ref, qi), rows(kb_ref, j))

    def weights(t, slot):
        qi, j = t
        z = z_ref[slot] + bias_ref[(qi == j).astype(jnp.int32)]
        carry, w = _sb_weights(z, carry_ref[qi], sfx_ref[...], stack_chunks=False)
        carry_ref[qi] = carry
        w_ref[slot] = w

    def value_product(t, slot):
        qi, j = t
        acct_ref[qi] += lax.dot_general(vt_ref[j], w_ref[slot], (((1,), (1,)), ((), ())),
                                        preferred_element_type=F32)

    def step(prev, cur, slot):
        nxt = following(cur)
        logits(nxt, 1 - slot)
        weights(cur, slot)
        value_product(prev, 1 - slot)
        return cur, nxt

    def group(i, st):
        prev, cur = (st[0], st[1]), (st[2], st[3])
        for s in range(ATTN_UNROLL):
            prev, cur = step(prev, cur, s % 2)
        return prev + cur

    zero = jnp.int32(0)
    logits((zero, zero), 0)
    st = lax.fori_loop(0, n_tiles // ATTN_UNROLL, group, (zero, zero, zero, zero))
    value_product((st[0], st[1]), 1)
    for qi in range(nq):
        o_ref[qi * tile:(qi + 1) * tile, :] = acct_ref[qi].T.astype(BF16)


def _attn_prompt(q, k, v, *, batch, seq):
    assert seq % Q_TILE == 0
    spec = pl.BlockSpec((seq, HEAD_DIM), lambda b, h: (b, h))
    return pl.pallas_call(
        functools.partial(_attn_prompt_kernel, seq=seq),
        grid=(batch, N_HEADS),
        in_specs=[spec, spec, spec],
        out_specs=spec,
        out_shape=jax.ShapeDtypeStruct((batch * seq, ATTN_WIDTH), BF16),
        scratch_shapes=[
            pltpu.VMEM((seq, HEAD_DIM), BF16),
            pltpu.VMEM((seq, HEAD_DIM), BF16),
            pltpu.VMEM((seq // Q_TILE, HEAD_DIM, Q_TILE), BF16),
            pltpu.VMEM((2 * LANES, 2 * LANES), BF16),
            pltpu.VMEM((2, Q_TILE, Q_TILE), F32),
            pltpu.VMEM((2, Q_TILE, Q_TILE), F32),
            pltpu.VMEM((2, Q_TILE, Q_TILE), BF16),
            pltpu.VMEM((seq // Q_TILE, Q_TILE, LANES), F32),
            pltpu.VMEM((seq // Q_TILE, HEAD_DIM, Q_TILE), F32),
        ],
        compiler_params=_params("arbitrary", "arbitrary"),
        name="attn_prompt",
    )(q, k, v)


def _attn_decode_kernel(q_ref, kn_ref, vn_ref, kp_ref, vp_ref, o_ref, sfx_ref, carry_ref, acc_ref,
                        *, rows):
    c = pl.program_id(1)

    def head_cols(h):
        return slice(h * HEAD_DIM, (h + 1) * HEAD_DIM)

    @pl.when(c == 0)
    def _():
        sfx_ref[...] = _suffix_matrix()
        bias = _causal_bias(rows, LANES)
        pad = jnp.zeros((LANES - rows, HEAD_DIM), BF16)
        for h in range(N_HEADS):
            q = q_ref[:, head_cols(h)].astype(BF16)
            kn = jnp.concatenate([kn_ref[:, head_cols(h)].astype(BF16), pad], axis=0)
            vn = jnp.concatenate([vn_ref[:, head_cols(h)].astype(BF16), pad], axis=0)
            carry, w = _sb_weights(_scores(q, kn) + bias, jnp.zeros((rows, LANES), F32),
                                   sfx_ref[...], stack_chunks=True)
            carry_ref[h] = carry
            acc_ref[h] = jnp.dot(w, vn, preferred_element_type=F32)

    for h in range(N_HEADS):
        q = q_ref[:, head_cols(h)].astype(BF16)
        k = kp_ref[pl.ds(h, DECODE_KEYS, stride=N_HEADS), :].astype(BF16)
        v = vp_ref[pl.ds(h, DECODE_KEYS, stride=N_HEADS), :].astype(BF16)
        carry, w = _sb_weights(_scores(q, k), carry_ref[h], sfx_ref[...], stack_chunks=True)
        carry_ref[h] = carry
        acc_ref[h] += jnp.dot(w, v, preferred_element_type=F32)

    @pl.when(c == pl.num_programs(1) - 1)
    def _():
        for h in range(N_HEADS):
            o_ref[:, head_cols(h)] = acc_ref[h].astype(BF16)


def _attn_decode(q, k_new, v_new, k_past, v_past, *, batch, rows, past):
    assert rows <= LANES and past % DECODE_KEYS == 0
    n_chunks = past // DECODE_KEYS
    new_spec = pl.BlockSpec((rows, ATTN_WIDTH), lambda b, c: (b, 0))
    past_spec = pl.BlockSpec((DECODE_KEYS * N_HEADS, HEAD_DIM),
                             lambda b, c: (b * n_chunks + n_chunks - 1 - c, 0))
    return pl.pallas_call(
        functools.partial(_attn_decode_kernel, rows=rows),
        grid=(batch, n_chunks),
        in_specs=[new_spec, new_spec, new_spec, past_spec, past_spec],
        out_specs=new_spec,
        out_shape=jax.ShapeDtypeStruct((batch * rows, ATTN_WIDTH), BF16),
        scratch_shapes=[pltpu.VMEM((2 * LANES, 2 * LANES), BF16),
                        pltpu.VMEM((N_HEADS, rows, LANES), F32),
                        pltpu.VMEM((N_HEADS, rows, HEAD_DIM), F32)],
        compiler_params=_params("arbitrary", "arbitrary"),
        name="attn_decode",
    )(q, k_new, v_new, k_past, v_past)


def _merge_kernel(oa_ref, ob_ref, wa_ref, wb_ref, ga_ref, gb_ref, out_ref):
    ya = jnp.dot(oa_ref[...], wa_ref[...], preferred_element_type=F32)
    yb = jnp.dot(ob_ref[...], wb_ref[...], preferred_element_type=F32)
    out_ref[...] = (_sigmoid(ga_ref[...]) * ya + _sigmoid(gb_ref[...]) * yb).astype(BF16)


def _merge(o_a, o_b, w_a, w_b, gates, *, tm, tn):
    n = o_a.shape[0]
    nj = D_MODEL // tn
    return pl.pallas_call(
        _merge_kernel,
        grid=(n // tm, nj),
        in_specs=[
            pl.BlockSpec((tm, POOL_WIDTH), lambda i, j: (i, 0)),
            pl.BlockSpec((tm, ATTN_WIDTH), lambda i, j: (i, 0)),
            pl.BlockSpec((POOL_WIDTH, tn), lambda i, j: (0, j)),
            pl.BlockSpec((ATTN_WIDTH, tn), lambda i, j: (0, j)),
            pl.BlockSpec((tm, tn), lambda i, j: (i, j)),
            pl.BlockSpec((tm, tn), lambda i, j: (i, nj + j)),
        ],
        out_specs=pl.BlockSpec((tm, tn), lambda i, j: (i, j)),
        out_shape=jax.ShapeDtypeStruct((n, D_MODEL), BF16),
        compiler_params=_params("arbitrary", "arbitrary"),
        name="merge",
    )(o_a, o_b, w_a, w_b, gates, gates)


def _outproj_kernel(m_ref, w_ref, x_ref, h_ref):
    h_ref[...] = x_ref[...] + jnp.dot(m_ref[...], w_ref[...], preferred_element_type=F32)


def _out_proj(merged, w_out, x, *, tm, tn):
    n = x.shape[0]
    return pl.pallas_call(
        _outproj_kernel,
        grid=(n // tm, D_MODEL // tn),
        in_specs=[
            pl.BlockSpec((tm, D_MODEL), lambda i, j: (i, 0)),
            pl.BlockSpec((D_MODEL, tn), lambda i, j: (0, j)),
            pl.BlockSpec((tm, tn), lambda i, j: (i, j)),
        ],
        out_specs=pl.BlockSpec((tm, tn), lambda i, j: (i, j)),
        out_shape=jax.ShapeDtypeStruct((n, D_MODEL), F32),
        compiler_params=_params("arbitrary", "arbitrary"),
        name="out_proj",
    )(merged, w_out, x)


def _ffn_up_kernel(h_ref, g_ref, wg_ref, wu_ref, hid_ref, n_ref):
    @pl.when(pl.program_id(1) == 0)
    def _():
        n_ref[...] = _rmsnorm(h_ref[...], g_ref[...]).astype(BF16)

    gate = jnp.dot(n_ref[...], wg_ref[...], preferred_element_type=F32)
    up = jnp.dot(n_ref[...], wu_ref[...], preferred_element_type=F32)
    hid_ref[...] = (gate * _sigmoid(gate) * up).astype(BF16)


def _ffn_up(h, g_ffn, w_gate_up, *, tm, tn):
    n = h.shape[0]
    nj = D_FF // tn
    return pl.pallas_call(
        _ffn_up_kernel,
        grid=(n // tm, nj),
        in_specs=[
            pl.BlockSpec((tm, D_MODEL), lambda i, j: (i, 0)),
            pl.BlockSpec((1, D_MODEL), lambda i, j: (0, 0)),
            pl.BlockSpec((D_MODEL, tn), lambda i, j: (0, j)),
            pl.BlockSpec((D_MODEL, tn), lambda i, j: (0, nj + j)),
        ],
        out_specs=pl.BlockSpec((tm, tn), lambda i, j: (i, j)),
        out_shape=jax.ShapeDtypeStruct((n, D_FF), BF16),
        scratch_shapes=[pltpu.VMEM((tm, D_MODEL), BF16)],
        compiler_params=_params("arbitrary", "arbitrary"),
        name="ffn_up",
    )(h, g_ffn, w_gate_up, w_gate_up)


def _ffn_down_kernel(hid_ref, w_ref, h_ref, g_ref, y_ref):
    k = pl.program_id(1)

    @pl.when(k == 0)
    def _():
        y_ref[...] = h_ref[...]

    y_ref[...] += jnp.dot(hid_ref[...], w_ref[...], preferred_element_type=F32)

    @pl.when(k == pl.num_programs(1) - 1)
    def _():
        y_ref[...] = _rmsnorm(y_ref[...], g_ref[...])


def _ffn_down(hid, w_down, h, g_final, *, tm, tk):
    n = h.shape[0]
    return pl.pallas_call(
        _ffn_down_kernel,
        grid=(n // tm, D_FF // tk),
        in_specs=[
            pl.BlockSpec((tm, tk), lambda i, k: (i, k)),
            pl.BlockSpec((tk, D_MODEL), lambda i, k: (k, 0)),
            pl.BlockSpec((tm, D_MODEL), lambda i, k: (i, 0)),
            pl.BlockSpec((1, D_MODEL), lambda i, k: (0, 0)),
        ],
        out_specs=pl.BlockSpec((tm, D_MODEL), lambda i, k: (i, 0)),
        out_shape=jax.ShapeDtypeStruct((n, D_MODEL), F32),
        compiler_params=_params("arbitrary", "arbitrary"),
        name="ffn_down",
    )(hid, w_down, h, g_final)


def _layer(x, hist, past_kv, weights, g_final, *, batch, seq, pos0, tm, pool_tm):
    g_mix, w_in, w_pool, s_pool, w_a, w_b, w_out, g_ffn, w_gate_up, w_down = weights
    u, q, k, v, gates = _in_proj(x, g_mix, w_in, tm=tm, tn=512)
    o_a = _pool(u, hist, w_pool, s_pool, batch=batch, seq=seq, tm=pool_tm, pos0=pos0)
    if past_kv is None:
        o_b = _attn_prompt(q, k, v, batch=batch, seq=seq)
    else:
        o_b = _attn_decode(q, k, v, past_kv[0], past_kv[1], batch=batch, rows=seq, past=pos0)
    merged = _merge(o_a, o_b, w_a, w_b, gates, tm=tm, tn=512)
    h = _out_proj(merged, w_out, x, tm=tm, tn=512)
    hid = _ffn_up(h, g_ffn, w_gate_up, tm=tm, tn=512)
    y = _ffn_down(hid, w_down, h, g_final, tm=tm, tk=512)
    return y, u, k, v


def kernel(x_prompt, x_sample, cache_k, cache_v, state_pool, g_mix, w_in, w_pool, s_pool,
           w_branch, w_out, g_ffn, w_gate_up, w_down, g_final):
    depth = w_in.shape[0]
    assert depth == 1
    b_p, t_p, _ = x_prompt.shape
    b_s, t_s, _ = x_sample.shape
    past = cache_k.shape[2]
    l = 0
    weights = (
        g_mix[l][None, :], w_in[l].astype(BF16), w_pool[l].astype(BF16), s_pool[l][None, :],
        w_branch[l, :POOL_WIDTH].astype(BF16), w_branch[l, POOL_WIDTH:].astype(BF16),
        w_out[l].astype(BF16), g_ffn[l][None, :], w_gate_up[l].astype(BF16),
        w_down[l].astype(BF16),
    )
    g_fin = g_final[None, :]

    hist_p = jnp.zeros((b_p, HIST_ROWS, POOL_WIDTH), F32)
    y_p, u_p, k_p, v_p = _layer(
        x_prompt.reshape(b_p * t_p, D_MODEL), hist_p, None, weights, g_fin,
        batch=b_p, seq=t_p, pos0=0, tm=1024, pool_tm=512)

    hist_s = jnp.pad(state_pool[l], ((0, 0), (HIST_ROWS - POOL_HIST, 0), (0, 0)))
    past_kv = (cache_k[l].reshape(b_s * past * N_HEADS, HEAD_DIM),
               cache_v[l].reshape(b_s * past * N_HEADS, HEAD_DIM))
    y_s, u_s, k_s, v_s = _layer(
        x_sample.reshape(b_s * t_s, D_MODEL), hist_s, past_kv, weights, g_fin,
        batch=b_s, seq=t_s, pos0=past, tm=b_s * t_s, pool_tm=t_s)

    new_pool_p = u_p.reshape(b_p, t_p, POOL_WIDTH)[:, t_p - POOL_HIST:]
    new_pool_s = jnp.concatenate([state_pool[l], u_s.reshape(b_s, t_s, POOL_WIDTH)],
                                 axis=1)[:, -POOL_HIST:]
    return (
        y_p.reshape(b_p, t_p, D_MODEL),
        y_s.reshape(b_s, t_s, D_MODEL),
        k_p.reshape(1, b_p, t_p, N_HEADS, HEAD_DIM),
        v_p.reshape(1, b_p, t_p, N_HEADS, HEAD_DIM),
        new_pool_p[None],
        k_s.reshape(1, b_s, t_s, N_HEADS, HEAD_DIM),
        v_s.reshape(1, b_s, t_s, N_HEADS, HEAD_DIM),
        new_pool_s[None],
    )
```

```python
import functools

import jax
import jax.numpy as jnp
from jax import lax
from jax.experimental import pallas as pl
from jax.experimental.pallas import tpu as pltpu

F32 = jnp.float32
BF16 = jnp.bfloat16

D_MODEL = 2048
N_HEADS = 8
HEAD_DIM = 128
ATTN_WIDTH = N_HEADS * HEAD_DIM
POOL_WINDOWS = (2, 4, 8, 16)
POOL_WIDTH = D_MODEL // 2
POOL_GROUP_WIDTH = POOL_WIDTH // len(POOL_WINDOWS)
POOL_HIST = max(POOL_WINDOWS) - 1
HIST_ROWS = POOL_HIST + 1
GROUP_WIDTH = 1024
N_GATE_GROUPS = 2 * D_MODEL // GROUP_WIDTH
D_FF = 5632
EPS = 1e-6

LANES = 128
CHUNK = 2 * LANES
DECODE_KEYS = 1024
Q_TILE = 512
Q_SCALE = 1.4426950408889634 * HEAD_DIM ** -0.5
VMEM_LIMIT = 56 * 1024 * 1024

ROW_BLOCK = 1024
WEIGHT_TILE = 512
MIX_ROWS = 512
MIX_COLS = 512
POOL_ROWS = 512


def _params(*sem):
    return pltpu.CompilerParams(dimension_semantics=sem, vmem_limit_bytes=VMEM_LIMIT)


def _rmsnorm(x, g):
    ms = jnp.mean(x * x, axis=-1, keepdims=True)
    return x * lax.rsqrt(ms + EPS) * g


def _sigmoid(x):
    return 1.0 / (1.0 + jnp.exp(-x))


def _inproj_kernel(x_ref, g_ref, w_ref, u_ref, q_ref, k_ref, v_ref, gate_ref, xn_ref, *, tpg):
    j = pl.program_id(1)

    @pl.when(j == 0)
    def _():
        xn_ref[...] = _rmsnorm(x_ref[...], g_ref[...]).astype(BF16)

    def proj():
        return jnp.dot(xn_ref[...], w_ref[...], preferred_element_type=F32)

    @pl.when(j < tpg)
    def _():
        u_ref[...] = proj()

    @pl.when((j >= tpg) & (j < 2 * tpg))
    def _():
        q_ref[...] = proj() * Q_SCALE

    @pl.when((j >= 2 * tpg) & (j < 3 * tpg))
    def _():
        k_ref[...] = proj()

    @pl.when((j >= 3 * tpg) & (j < 4 * tpg))
    def _():
        v_ref[...] = proj()

    @pl.when(j >= 4 * tpg)
    def _():
        gate_ref[...] = proj()


def _in_proj(x, g_mix, w_in, *, tm):
    n = x.shape[0]
    n_tiles, _, tn = w_in.shape
    tpg = GROUP_WIDTH // tn

    def group_spec(g, ntiles):
        return pl.BlockSpec((tm, tn), lambda i, j: (i, jnp.clip(j - g * tpg, 0, ntiles - 1)))

    small = jax.ShapeDtypeStruct((n, GROUP_WIDTH), F32)
    return pl.pallas_call(
        functools.partial(_inproj_kernel, tpg=tpg),
        grid=(n // tm, n_tiles),
        in_specs=[
            pl.BlockSpec((tm, D_MODEL), lambda i, j: (i, 0)),
            pl.BlockSpec((1, D_MODEL), lambda i, j: (0, 0)),
            pl.BlockSpec((None, D_MODEL, tn), lambda i, j: (j, 0, 0)),
        ],
        out_specs=[group_spec(0, tpg), group_spec(1, tpg), group_spec(2, tpg), group_spec(3, tpg),
                   group_spec(4, N_GATE_GROUPS * tpg)],
        out_shape=[small, small, small, small,
                   jax.ShapeDtypeStruct((n, N_GATE_GROUPS * GROUP_WIDTH), F32)],
        scratch_shapes=[pltpu.VMEM((tm, D_MODEL), BF16)],
        compiler_params=_params("arbitrary", "arbitrary"),
        name="in_proj",
    )(x, g_mix, w_in)


def _pool_kernel(u_ref, uprev_ref, hist_ref, wp_ref, sp_ref, o_ref, buf_ref, *, tm, pos0):
    i = pl.program_id(1)
    buf_ref[HIST_ROWS:, :] = u_ref[...]

    @pl.when(i == 0)
    def _():
        buf_ref[:HIST_ROWS, :] = hist_ref[0]

    @pl.when(i > 0)
    def _():
        buf_ref[:HIST_ROWS, :] = uprev_ref[...]

    pos = pos0 + i * tm + lax.broadcasted_iota(jnp.int32, (tm, 1), 0)
    for g, w in enumerate(POOL_WINDOWS):
        cols = slice(g * POOL_GROUP_WIDTH, (g + 1) * POOL_GROUP_WIDTH)
        cur = buf_ref[HIST_ROWS:, cols]
        s = cur
        for d in range(1, w):
            s = s + buf_ref[HIST_ROWS - d:HIST_ROWS - d + tm, cols]
        cnt = jnp.minimum(pos + 1, w).astype(F32)
        diff = s / cnt - cur
        o = jnp.dot(diff.astype(BF16), wp_ref[g], preferred_element_type=F32)
        o_ref[:, cols] = (o * sp_ref[:, cols]).astype(BF16)


def _pool(u, hist, w_pool, s_pool, *, batch, seq, tm, pos0):
    nt = seq // tm
    per = tm // HIST_ROWS
    return pl.pallas_call(
        functools.partial(_pool_kernel, tm=tm, pos0=pos0),
        grid=(batch, nt),
        in_specs=[
            pl.BlockSpec((tm, POOL_WIDTH), lambda b, i: (b * nt + i, 0)),
            pl.BlockSpec((HIST_ROWS, POOL_WIDTH),
                         lambda b, i: (jnp.maximum((b * nt + i) * per - 1, 0), 0)),
            pl.BlockSpec((1, HIST_ROWS, POOL_WIDTH), lambda b, i: (b, 0, 0)),
            pl.BlockSpec(w_pool.shape, lambda b, i: (0, 0, 0)),
            pl.BlockSpec((1, POOL_WIDTH), lambda b, i: (0, 0)),
        ],
        out_specs=pl.BlockSpec((tm, POOL_WIDTH), lambda b, i: (b * nt + i, 0)),
        out_shape=jax.ShapeDtypeStruct((batch * seq, POOL_WIDTH), BF16),
        scratch_shapes=[pltpu.VMEM((HIST_ROWS + tm, POOL_WIDTH), F32)],
        compiler_params=_params("arbitrary", "arbitrary"),
        name="pool",
    )(u, u, hist, w_pool, s_pool)


def _suffix_matrix():
    r = lax.broadcasted_iota(jnp.int32, (CHUNK, CHUNK), 0)
    c = lax.broadcasted_iota(jnp.int32, (CHUNK, CHUNK), 1)
    return jnp.where(r >= c, 1.0, 0.0).astype(BF16)


def _scores(q, kblk):
    return lax.dot_general(q, kblk, (((1,), (1,)), ((), ())), preferred_element_type=F32)


def _sb_weights(z, carry, sfx, *, stack_chunks):
    rows, span = z.shape
    n_chunks = span // CHUNK
    sp = (jnp.maximum(z, 0.0) + jnp.log2(1.0 + jnp.exp2(-jnp.abs(z)))).astype(BF16)

    def chunk(x, c):
        return x[:, c * CHUNK:(c + 1) * CHUNK]

    if stack_chunks:
        r_all = jnp.dot(jnp.concatenate([chunk(sp, c) for c in range(n_chunks)], axis=0), sfx,
                        preferred_element_type=F32)
    ws = []
    for c in reversed(range(n_chunks)):
        if stack_chunks:
            r = r_all[c * rows:(c + 1) * rows]
        else:
            r = jnp.dot(chunk(sp, c), sfx, preferred_element_type=F32)
        w = jnp.exp2(chunk(z, c) - (r + jnp.concatenate([carry] * (CHUNK // LANES), axis=1)))
        ws.insert(0, w.astype(BF16))
        carry = carry + jnp.broadcast_to(r[:, :1], (rows, LANES))
    return carry, jnp.concatenate(ws, axis=1)


def _causal_bias(rows, span):
    r = lax.broadcasted_iota(jnp.int32, (rows, span), 0)
    c = lax.broadcasted_iota(jnp.int32, (rows, span), 1)
    return jnp.where(c < r, 0.0, MASKED_LOGIT)


ATTN_UNROLL = 2
MASKED_LOGIT = -1e9


def _attn_prompt_kernel(q_ref, k_ref, v_ref, o_ref, qb_ref, kb_ref, vt_ref, sfx_ref, bias_ref,
                        z_ref, w_ref, carry_ref, acct_ref, *, seq):
    tile = Q_TILE
    nq = seq // tile
    n_tiles = nq * (nq + 1) // 2
    assert ATTN_UNROLL % 2 == 0 and n_tiles % ATTN_UNROLL == 0
    qb_ref[...] = q_ref[...].astype(BF16)
    kb_ref[...] = k_ref[...].astype(BF16)
    for j in range(nq):
        vt_ref[j] = v_ref[j * tile:(j + 1) * tile, :].T.astype(BF16)

    @pl.when((pl.program_id(0) == 0) & (pl.program_id(1) == 0))
    def _():
        sfx_ref[...] = _suffix_matrix()
        bias_ref[0] = jnp.zeros((tile, tile), F32)
        bias_ref[1] = _causal_bias(tile, tile)

    carry_ref[...] = jnp.zeros(carry_ref.shape, F32)
    acct_ref[...] = jnp.zeros(acct_ref.shape, F32)
    w_ref[1] = jnp.zeros((tile, tile), BF16)

    def rows(ref, i):
        return ref[pl.ds(pl.multiple_of(i * tile, tile), tile), :]

    def following(t):
        qi, j = t
        wrap = j == 0
        return jnp.where(wrap, qi + 1, qi), jnp.where(wrap, qi + 1, j - 1)

    def logits(t, slot):
        qi, j = jnp.minimum(t[0], nq - 1), jnp.minimum(t[1], nq - 1)
        z_ref[slot] = _scores(rows(qb_ref, qi), rows(kb_ref, j))

    def weights(t, slot):
        qi, j = t
        z = z_ref[slot] + bias_ref[(qi == j).astype(jnp.int32)]
        carry, w = _sb_weights(z, carry_ref[qi], sfx_ref[...], stack_chunks=False)
        carry_ref[qi] = carry
        w_ref[slot] = w

    def value_product(t, slot):
        qi, j = t
        acct_ref[qi] += lax.dot_general(vt_ref[j], w_ref[slot], (((1,), (1,)), ((), ())),
                                        preferred_element_type=F32)

    def step(prev, cur, slot):
        nxt = following(cur)
        logits(nxt, 1 - slot)
        weights(cur, slot)
        value_product(prev, 1 - slot)
        return cur, nxt

    def group(i, st):
        prev, cur = (st[0], st[1]), (st[2], st[3])
        for s in range(ATTN_UNROLL):
            prev, cur = step(prev, cur, s % 2)
        return prev + cur

    zero = jnp.int32(0)
    logits((zero, zero), 0)
    st = lax.fori_loop(0, n_tiles // ATTN_UNROLL, group, (zero, zero, zero, zero))
    value_product((st[0], st[1]), 1)
    for qi in range(nq):
        o_ref[qi * tile:(qi + 1) * tile, :] = acct_ref[qi].T.astype(BF16)


def _attn_prompt(q, k, v, *, batch, seq):
    assert seq % Q_TILE == 0
    spec = pl.BlockSpec((seq, HEAD_DIM), lambda b, h: (b, h))
    return pl.pallas_call(
        functools.partial(_attn_prompt_kernel, seq=seq),
        grid=(batch, N_HEADS),
        in_specs=[spec, spec, spec],
        out_specs=spec,
        out_shape=jax.ShapeDtypeStruct((batch * seq, ATTN_WIDTH), BF16),
        scratch_shapes=[
            pltpu.VMEM((seq, HEAD_DIM), BF16),
            pltpu.VMEM((seq, HEAD_DIM), BF16),
            pltpu.VMEM((seq // Q_TILE, HEAD_DIM, Q_TILE), BF16),
            pltpu.VMEM((CHUNK, CHUNK), BF16),
            pltpu.VMEM((2, Q_TILE, Q_TILE), F32),
            pltpu.VMEM((2, Q_TILE, Q_TILE), F32),
            pltpu.VMEM((2, Q_TILE, Q_TILE), BF16),
            pltpu.VMEM((seq // Q_TILE, Q_TILE, LANES), F32),
            pltpu.VMEM((seq // Q_TILE, HEAD_DIM, Q_TILE), F32),
        ],
        compiler_params=_params("arbitrary", "arbitrary"),
        name="attn_prompt",
    )(q, k, v)


def _attn_decode_kernel(q_ref, kn_ref, vn_ref, kp_ref, vp_ref, o_ref, sfx_ref, carry_ref, acc_ref,
                        *, rows):
    c = pl.program_id(1)

    def head_cols(h):
        return slice(h * HEAD_DIM, (h + 1) * HEAD_DIM)

    @pl.when(c == 0)
    def _():
        sfx_ref[...] = _suffix_matrix()
        bias = _causal_bias(rows, CHUNK)
        pad = jnp.zeros((CHUNK - rows, HEAD_DIM), BF16)
        for h in range(N_HEADS):
            q = q_ref[:, head_cols(h)].astype(BF16)
            kn = jnp.concatenate([kn_ref[:, head_cols(h)].astype(BF16), pad], axis=0)
            vn = jnp.concatenate([vn_ref[:, head_cols(h)].astype(BF16), pad], axis=0)
            carry, w = _sb_weights(_scores(q, kn) + bias, jnp.zeros((rows, LANES), F32),
                                   sfx_ref[...], stack_chunks=True)
            carry_ref[h] = carry
            acc_ref[h] = jnp.dot(w, vn, preferred_element_type=F32)

    for h in range(N_HEADS):
        q = q_ref[:, head_cols(h)].astype(BF16)
        k = kp_ref[pl.ds(h, DECODE_KEYS, stride=N_HEADS), :].astype(BF16)
        v = vp_ref[pl.ds(h, DECODE_KEYS, stride=N_HEADS), :].astype(BF16)
        carry, w = _sb_weights(_scores(q, k), carry_ref[h], sfx_ref[...], stack_chunks=True)
        carry_ref[h] = carry
        acc_ref[h] += jnp.dot(w, v, preferred_element_type=F32)

    @pl.when(c == pl.num_programs(1) - 1)
    def _():
        for h in range(N_HEADS):
            o_ref[:, head_cols(h)] = acc_ref[h].astype(BF16)


def _attn_decode(q, k_new, v_new, k_past, v_past, *, batch, rows, past):
    assert rows <= LANES and past % DECODE_KEYS == 0
    n_chunks = past // DECODE_KEYS
    new_spec = pl.BlockSpec((rows, ATTN_WIDTH), lambda b, c: (b, 0))
    past_spec = pl.BlockSpec((DECODE_KEYS * N_HEADS, HEAD_DIM),
                             lambda b, c: (b * n_chunks + n_chunks - 1 - c, 0))
    return pl.pallas_call(
        functools.partial(_attn_decode_kernel, rows=rows),
        grid=(batch, n_chunks),
        in_specs=[new_spec, new_spec, new_spec, past_spec, past_spec],
        out_specs=new_spec,
        out_shape=jax.ShapeDtypeStruct((batch * rows, ATTN_WIDTH), BF16),
        scratch_shapes=[pltpu.VMEM((CHUNK, CHUNK), BF16),
                        pltpu.VMEM((N_HEADS, rows, LANES), F32),
                        pltpu.VMEM((N_HEADS, rows, HEAD_DIM), F32)],
        compiler_params=_params("arbitrary", "arbitrary"),
        name="attn_decode",
    )(q, k_new, v_new, k_past, v_past)


def _mix_out_kernel(oa_ref, ob_ref, gates_ref, x_ref, wa_ref, wb_ref, wo_ref, h_ref, m_ref):
    chunks = [slice(c * MIX_COLS, (c + 1) * MIX_COLS) for c in range(D_MODEL // MIX_COLS)]
    for cols in chunks:
        gate_b_cols = slice(D_MODEL + cols.start, D_MODEL + cols.stop)
        ya = jnp.dot(oa_ref[...], wa_ref[:, cols], preferred_element_type=F32)
        yb = jnp.dot(ob_ref[...], wb_ref[:, cols], preferred_element_type=F32)
        m_ref[:, cols] = (_sigmoid(gates_ref[:, cols]) * ya
                          + _sigmoid(gates_ref[:, gate_b_cols]) * yb).astype(BF16)
    for cols in chunks:
        h_ref[:, cols] = x_ref[:, cols] + jnp.dot(m_ref[...], wo_ref[:, cols],
                                                  preferred_element_type=F32)


def _mix_out(o_a, o_b, gates, x, w_a, w_b, w_out, *, tm):
    n = x.shape[0]

    def rows(width):
        return pl.BlockSpec((tm, width), lambda i: (i, 0))

    def resident(w):
        return pl.BlockSpec(w.shape, lambda i: (0, 0), pipeline_mode=pl.Buffered(1))

    return pl.pallas_call(
        _mix_out_kernel,
        grid=(n // tm,),
        in_specs=[rows(POOL_WIDTH), rows(ATTN_WIDTH), rows(2 * D_MODEL), rows(D_MODEL),
                  resident(w_a), resident(w_b), resident(w_out)],
        out_specs=rows(D_MODEL),
        out_shape=jax.ShapeDtypeStruct((n, D_MODEL), F32),
        scratch_shapes=[pltpu.VMEM((tm, D_MODEL), BF16)],
        compiler_params=_params("arbitrary"),
        name="mix_out",
    )(o_a, o_b, gates, x, w_a, w_b, w_out)


def _ffn_up_kernel(h_ref, g_ref, wg_ref, wu_ref, hid_ref, n_ref):
    @pl.when(pl.program_id(1) == 0)
    def _():
        n_ref[...] = _rmsnorm(h_ref[...], g_ref[...]).astype(BF16)

    gate = jnp.dot(n_ref[...], wg_ref[...], preferred_element_type=F32)
    up = jnp.dot(n_ref[...], wu_ref[...], preferred_element_type=F32)
    hid_ref[...] = (gate * _sigmoid(gate) * up).astype(BF16)


def _ffn_up(h, g_ffn, w_gate_up, *, tm):
    n = h.shape[0]
    n_tiles, _, tn = w_gate_up.shape
    nj = n_tiles // 2
    return pl.pallas_call(
        _ffn_up_kernel,
        grid=(n // tm, nj),
        in_specs=[
            pl.BlockSpec((tm, D_MODEL), lambda i, j: (i, 0)),
            pl.BlockSpec((1, D_MODEL), lambda i, j: (0, 0)),
            pl.BlockSpec((None, D_MODEL, tn), lambda i, j: (j, 0, 0)),
            pl.BlockSpec((None, D_MODEL, tn), lambda i, j: (nj + j, 0, 0)),
        ],
        out_specs=pl.BlockSpec((tm, tn), lambda i, j: (i, j)),
        out_shape=jax.ShapeDtypeStruct((n, D_FF), BF16),
        scratch_shapes=[pltpu.VMEM((tm, D_MODEL), BF16)],
        compiler_params=_params("arbitrary", "arbitrary"),
        name="ffn_up",
    )(h, g_ffn, w_gate_up, w_gate_up)


def _ffn_down_kernel(hid_ref, w_ref, h_ref, g_ref, y_ref):
    k = pl.program_id(1)

    @pl.when(k == 0)
    def _():
        y_ref[...] = h_ref[...]

    y_ref[...] += jnp.dot(hid_ref[...], w_ref[...], preferred_element_type=F32)

    @pl.when(k == pl.num_programs(1) - 1)
    def _():
        y_ref[...] = _rmsnorm(y_ref[...], g_ref[...])


def _ffn_down(hid, w_down, h, g_final, *, tm, tk):
    n = h.shape[0]
    return pl.pallas_call(
        _ffn_down_kernel,
        grid=(n // tm, D_FF // tk),
        in_specs=[
            pl.BlockSpec((tm, tk), lambda i, k: (i, k)),
            pl.BlockSpec((tk, D_MODEL), lambda i, k: (k, 0)),
            pl.BlockSpec((tm, D_MODEL), lambda i, k: (i, 0)),
            pl.BlockSpec((1, D_MODEL), lambda i, k: (0, 0)),
        ],
        out_specs=pl.BlockSpec((tm, D_MODEL), lambda i, k: (i, 0)),
        out_shape=jax.ShapeDtypeStruct((n, D_MODEL), F32),
        compiler_params=_params("arbitrary", "arbitrary"),
        name="ffn_down",
    )(hid, w_down, h, g_final)


def _col_tiles(w):
    k, n = w.shape
    return w.astype(BF16).reshape(k, n // WEIGHT_TILE, WEIGHT_TILE).transpose(1, 0, 2)


def _layer(x, hist, past_kv, weights, g_final, *, batch, seq, pos0):
    g_mix, w_in, w_pool, s_pool, w_a, w_b, w_out, g_ffn, w_gate_up, w_down = weights
    n = batch * seq
    tm = min(n, ROW_BLOCK)
    u, q, k, v, gates = _in_proj(x, g_mix, w_in, tm=tm)
    o_a = _pool(u, hist, w_pool, s_pool, batch=batch, seq=seq, tm=min(seq, POOL_ROWS), pos0=pos0)
    if past_kv is None:
        o_b = _attn_prompt(q, k, v, batch=batch, seq=seq)
    else:
        o_b = _attn_decode(q, k, v, past_kv[0], past_kv[1], batch=batch, rows=seq, past=pos0)
    h = _mix_out(o_a, o_b, gates, x, w_a, w_b, w_out, tm=min(n, MIX_ROWS))
    hid = _ffn_up(h, g_ffn, w_gate_up, tm=tm)
    y = _ffn_down(hid, w_down, h, g_final, tm=tm, tk=WEIGHT_TILE)
    return y, u, k, v


def kernel(x_prompt, x_sample, cache_k, cache_v, state_pool, g_mix, w_in, w_pool, s_pool,
           w_branch, w_out, g_ffn, w_gate_up, w_down, g_final):
    depth = w_in.shape[0]
    assert depth == 1
    b_p, t_p, _ = x_prompt.shape
    b_s, t_s, _ = x_sample.shape
    past = cache_k.shape[2]
    l = 0
    weights = (
        g_mix[l][None, :], _col_tiles(w_in[l]), w_pool[l].astype(BF16), s_pool[l][None, :],
        w_branch[l, :POOL_WIDTH].astype(BF16), w_branch[l, POOL_WIDTH:].astype(BF16),
        w_out[l].astype(BF16), g_ffn[l][None, :], _col_tiles(w_gate_up[l]),
        w_down[l].astype(BF16),
    )
    g_fin = g_final[None, :]

    hist_p = jnp.zeros((b_p, HIST_ROWS, POOL_WIDTH), F32)
    y_p, u_p, k_p, v_p = _layer(
        x_prompt.reshape(b_p * t_p, D_MODEL), hist_p, None, weights, g_fin,
        batch=b_p, seq=t_p, pos0=0)

    hist_s = jnp.pad(state_pool[l], ((0, 0), (HIST_ROWS - POOL_HIST, 0), (0, 0)))
    past_kv = (cache_k[l].reshape(b_s * past * N_HEADS, HEAD_DIM),
               cache_v[l].reshape(b_s * past * N_HEADS, HEAD_DIM))
    y_s, u_s, k_s, v_s = _layer(
        x_sample.reshape(b_s * t_s, D_MODEL), hist_s, past_kv, weights, g_fin,
        batch=b_s, seq=t_s, pos0=past)

    new_pool_p = u_p.reshape(b_p, t_p, POOL_WIDTH)[:, t_p - POOL_HIST:]
    new_pool_s = jnp.concatenate([state_pool[l], u_s.reshape(b_s, t_s, POOL_WIDTH)],
                                 axis=1)[:, -POOL_HIST:]
    return (
        y_p.reshape(b_p, t_p, D_MODEL),
        y_s.reshape(b_s, t_s, D_MODEL),
        k_p.reshape(1, b_p, t_p, N_HEADS, HEAD_DIM),
        v_p.reshape(1, b_p, t_p, N_HEADS, HEAD_DIM),
        new_pool_p[None],
        k_s.reshape(1, b_s, t_s, N_HEADS, HEAD_DIM),
        v_s.reshape(1, b_s, t_s, N_HEADS, HEAD_DIM),
        new_pool_s[None],
    )
```

```python
import functools

import jax
import jax.numpy as jnp
from jax import lax
from jax.experimental import pallas as pl
from jax.experimental.pallas import tpu as pltpu

F32 = jnp.float32
BF16 = jnp.bfloat16

D_MODEL = 2048
N_HEADS = 8
HEAD_DIM = 128
ATTN_WIDTH = N_HEADS * HEAD_DIM
POOL_WINDOWS = (2, 4, 8, 16)
POOL_WIDTH = D_MODEL // 2
POOL_GROUP_WIDTH = POOL_WIDTH // len(POOL_WINDOWS)
POOL_HIST = max(POOL_WINDOWS) - 1
HIST_ROWS = POOL_HIST + 1
GROUP_WIDTH = 1024
N_GATE_GROUPS = 2 * D_MODEL // GROUP_WIDTH
D_FF = 5632
EPS = 1e-6

LANES = 128
CHUNK = 2 * LANES
DECODE_KEYS = 1024
Q_TILE = 512
Q_SCALE = 1.4426950408889634 * HEAD_DIM ** -0.5
VMEM_LIMIT = 56 * 1024 * 1024

ROW_BLOCK = 1024
WEIGHT_TILE = 512
MIX_ROWS = 512
MIX_COLS = 512
POOL_ROWS = 512


def _params(*sem):
    return pltpu.CompilerParams(dimension_semantics=sem, vmem_limit_bytes=VMEM_LIMIT)


def _rmsnorm(x, g):
    ms = jnp.mean(x * x, axis=-1, keepdims=True)
    return x * lax.rsqrt(ms + EPS) * g


def _sigmoid(x):
    return 1.0 / (1.0 + jnp.exp(-x))


def _inproj_kernel(x_ref, g_ref, w_ref, u_ref, q_ref, k_ref, v_ref, gate_ref, xn_ref, *, tpg):
    j = pl.program_id(1)

    @pl.when(j == 0)
    def _():
        xn_ref[...] = _rmsnorm(x_ref[...], g_ref[...]).astype(BF16)

    def proj():
        return jnp.dot(xn_ref[...], w_ref[...], preferred_element_type=F32)

    @pl.when(j < tpg)
    def _():
        u_ref[...] = proj()

    @pl.when((j >= tpg) & (j < 2 * tpg))
    def _():
        q_ref[...] = (proj() * Q_SCALE).astype(BF16)

    @pl.when((j >= 2 * tpg) & (j < 3 * tpg))
    def _():
        k_ref[...] = proj()

    @pl.when((j >= 3 * tpg) & (j < 4 * tpg))
    def _():
        v_ref[...] = proj()

    @pl.when(j >= 4 * tpg)
    def _():
        gate_ref[...] = proj()


def _in_proj(x, g_mix, w_in, *, tm):
    n = x.shape[0]
    tn = WEIGHT_TILE
    tpg = GROUP_WIDTH // tn
    n_tiles = w_in.shape[1] // tn
    n_blocks = n // tm

    def group_spec(first, ntiles):
        def index(i, j):
            moved_on = (j >= first + ntiles) & (i + 1 < n_blocks)
            return (jnp.where(moved_on, i + 1, i),
                    jnp.where(moved_on, 0, jnp.clip(j - first, 0, ntiles - 1)))

        return pl.BlockSpec((tm, tn), index)

    def group_shape(dtype):
        return jax.ShapeDtypeStruct((n, GROUP_WIDTH), dtype)

    return pl.pallas_call(
        functools.partial(_inproj_kernel, tpg=tpg),
        grid=(n_blocks, n_tiles),
        in_specs=[
            pl.BlockSpec((tm, D_MODEL),
                         lambda i, j: (jnp.where(j > 0, jnp.minimum(i + 1, n_blocks - 1), i), 0)),
            pl.BlockSpec((1, D_MODEL), lambda i, j: (0, 0)),
            pl.BlockSpec((D_MODEL, tn), lambda i, j: (0, j)),
        ],
        out_specs=[group_spec(0, tpg), group_spec(tpg, tpg), group_spec(2 * tpg, tpg),
                   group_spec(3 * tpg, tpg), group_spec(4 * tpg, N_GATE_GROUPS * tpg)],
        out_shape=[group_shape(F32), group_shape(BF16), group_shape(F32), group_shape(F32),
                   jax.ShapeDtypeStruct((n, N_GATE_GROUPS * GROUP_WIDTH), F32)],
        scratch_shapes=[pltpu.VMEM((tm, D_MODEL), BF16)],
        compiler_params=_params("arbitrary", "arbitrary"),
        name="in_proj",
    )(x, g_mix, w_in)


def _pool_kernel(u_ref, uprev_ref, hist_ref, wp_ref, sp_ref, o_ref, buf_ref, *, tm, pos0):
    i = pl.program_id(1)
    buf_ref[HIST_ROWS:, :] = u_ref[...]

    @pl.when(i == 0)
    def _():
        buf_ref[:HIST_ROWS, :] = hist_ref[0]

    @pl.when(i > 0)
    def _():
        buf_ref[:HIST_ROWS, :] = uprev_ref[...]

    pos = pos0 + i * tm + lax.broadcasted_iota(jnp.int32, (tm, 1), 0)
    for g, w in enumerate(POOL_WINDOWS):
        cols = slice(g * POOL_GROUP_WIDTH, (g + 1) * POOL_GROUP_WIDTH)
        cur = buf_ref[HIST_ROWS:, cols]
        s = cur
        for d in range(1, w):
            s = s + buf_ref[HIST_ROWS - d:HIST_ROWS - d + tm, cols]
        cnt = jnp.minimum(pos + 1, w).astype(F32)
        diff = s / cnt - cur
        o = jnp.dot(diff.astype(BF16), wp_ref[g], preferred_element_type=F32)
        o_ref[:, cols] = (o * sp_ref[:, cols]).astype(BF16)


def _pool(u, hist, w_pool, s_pool, *, batch, seq, tm, pos0):
    nt = seq // tm
    per = tm // HIST_ROWS
    return pl.pallas_call(
        functools.partial(_pool_kernel, tm=tm, pos0=pos0),
        grid=(batch, nt),
        in_specs=[
            pl.BlockSpec((tm, POOL_WIDTH), lambda b, i: (b * nt + i, 0)),
            pl.BlockSpec((HIST_ROWS, POOL_WIDTH),
                         lambda b, i: (jnp.maximum((b * nt + i) * per - 1, 0), 0)),
            pl.BlockSpec((1, HIST_ROWS, POOL_WIDTH), lambda b, i: (b, 0, 0)),
            pl.BlockSpec(w_pool.shape, lambda b, i: (0, 0, 0)),
            pl.BlockSpec((1, POOL_WIDTH), lambda b, i: (0, 0)),
        ],
        out_specs=pl.BlockSpec((tm, POOL_WIDTH), lambda b, i: (b * nt + i, 0)),
        out_shape=jax.ShapeDtypeStruct((batch * seq, POOL_WIDTH), BF16),
        scratch_shapes=[pltpu.VMEM((HIST_ROWS + tm, POOL_WIDTH), F32)],
        compiler_params=_params("arbitrary", "arbitrary"),
        name="pool",
    )(u, u, hist, w_pool, s_pool)


def _suffix_matrix():
    r = lax.broadcasted_iota(jnp.int32, (CHUNK, CHUNK), 0)
    c = lax.broadcasted_iota(jnp.int32, (CHUNK, CHUNK), 1)
    return jnp.where(r >= c, 1.0, 0.0).astype(BF16)


def _scores(q, kblk):
    return lax.dot_general(q, kblk, (((1,), (1,)), ((), ())), preferred_element_type=F32)


def _sb_weights(z, carry, sfx, *, stack_chunks):
    rows, span = z.shape
    n_chunks = span // CHUNK
    sp = (jnp.maximum(z, 0.0) + jnp.log2(1.0 + jnp.exp2(-jnp.abs(z)))).astype(BF16)

    def chunk(x, c):
        return x[:, c * CHUNK:(c + 1) * CHUNK]

    if stack_chunks:
        r_all = jnp.dot(jnp.concatenate([chunk(sp, c) for c in range(n_chunks)], axis=0), sfx,
                        preferred_element_type=F32)
    ws = []
    for c in reversed(range(n_chunks)):
        if stack_chunks:
            r = r_all[c * rows:(c + 1) * rows]
        else:
            r = jnp.dot(chunk(sp, c), sfx, preferred_element_type=F32)
        w = jnp.exp2(chunk(z, c) - (r + jnp.concatenate([carry] * (CHUNK // LANES), axis=1)))
        ws.insert(0, w.astype(BF16))
        carry = carry + jnp.broadcast_to(r[:, :1], (rows, LANES))
    return carry, jnp.concatenate(ws, axis=1)


def _causal_bias(rows, span):
    r = lax.broadcasted_iota(jnp.int32, (rows, span), 0)
    c = lax.broadcasted_iota(jnp.int32, (rows, span), 1)
    return jnp.where(c < r, 0.0, MASKED_LOGIT)


ATTN_UNROLL = 2
MASKED_LOGIT = -1e9


def _attn_prompt_kernel(q_ref, k_ref, v_ref, o_ref, kb_ref, vt_ref, sfx_ref, bias_ref,
                        z_ref, w_ref, carry_ref, acct_ref, *, seq):
    tile = Q_TILE
    nq = seq // tile
    n_tiles = nq * (nq + 1) // 2
    assert ATTN_UNROLL % 2 == 0 and n_tiles % ATTN_UNROLL == 0
    kb_ref[...] = k_ref[...].astype(BF16)
    for j in range(nq):
        vt_ref[j] = v_ref[j * tile:(j + 1) * tile, :].T.astype(BF16)

    @pl.when((pl.program_id(0) == 0) & (pl.program_id(1) == 0))
    def _():
        sfx_ref[...] = _suffix_matrix()
        bias_ref[0] = jnp.zeros((tile, tile), F32)
        bias_ref[1] = _causal_bias(tile, tile)

    carry_ref[...] = jnp.zeros(carry_ref.shape, F32)
    acct_ref[...] = jnp.zeros(acct_ref.shape, F32)
    w_ref[1] = jnp.zeros((tile, tile), BF16)

    def rows(ref, i):
        return ref[pl.ds(pl.multiple_of(i * tile, tile), tile), :]

    def following(t):
        qi, j = t
        wrap = j == 0
        return jnp.where(wrap, qi + 1, qi), jnp.where(wrap, qi + 1, j - 1)

    def logits(t, slot):
        qi, j = jnp.minimum(t[0], nq - 1), jnp.minimum(t[1], nq - 1)
        z_ref[slot] = _scores(rows(q_ref, qi), rows(kb_ref, j))

    def weights(t, slot):
        qi, j = t
        z = z_ref[slot] + bias_ref[(qi == j).astype(jnp.int32)]
        carry, w = _sb_weights(z, carry_ref[qi], sfx_ref[...], stack_chunks=False)
        carry_ref[qi] = carry
        w_ref[slot] = w

    def value_product(t, slot):
        qi, j = t
        acct_ref[qi] += lax.dot_general(vt_ref[j], w_ref[slot], (((1,), (1,)), ((), ())),
                                        preferred_element_type=F32)

    def step(prev, cur, slot):
        nxt = following(cur)
        logits(nxt, 1 - slot)
        weights(cur, slot)
        value_product(prev, 1 - slot)
        return cur, nxt

    def group(i, st):
        prev, cur = (st[0], st[1]), (st[2], st[3])
        for s in range(ATTN_UNROLL):
            prev, cur = step(prev, cur, s % 2)
        return prev + cur

    zero = jnp.int32(0)
    logits((zero, zero), 0)
    st = lax.fori_loop(0, n_tiles // ATTN_UNROLL, group, (zero, zero, zero, zero))
    value_product((st[0], st[1]), 1)
    for qi in range(nq):
        o_ref[qi * tile:(qi + 1) * tile, :] = acct_ref[qi].T.astype(BF16)


def _attn_prompt(q, k, v, *, batch, seq):
    assert seq % Q_TILE == 0
    spec = pl.BlockSpec((seq, HEAD_DIM), lambda b, h: (b, h))
    return pl.pallas_call(
        functools.partial(_attn_prompt_kernel, seq=seq),
        grid=(batch, N_HEADS),
        in_specs=[spec, spec, spec],
        out_specs=spec,
        out_shape=jax.ShapeDtypeStruct((batch * seq, ATTN_WIDTH), BF16),
        scratch_shapes=[
            pltpu.VMEM((seq, HEAD_DIM), BF16),
            pltpu.VMEM((seq // Q_TILE, HEAD_DIM, Q_TILE), BF16),
            pltpu.VMEM((CHUNK, CHUNK), BF16),
            pltpu.VMEM((2, Q_TILE, Q_TILE), F32),
            pltpu.VMEM((2, Q_TILE, Q_TILE), F32),
            pltpu.VMEM((2, Q_TILE, Q_TILE), BF16),
            pltpu.VMEM((seq // Q_TILE, Q_TILE, LANES), F32),
            pltpu.VMEM((seq // Q_TILE, HEAD_DIM, Q_TILE), F32),
        ],
        compiler_params=_params("arbitrary", "arbitrary"),
        name="attn_prompt",
    )(q, k, v)


def _attn_decode_kernel(q_ref, kn_ref, vn_ref, kp_ref, vp_ref, o_ref, sfx_ref, carry_ref, acc_ref,
                        *, rows):
    c = pl.program_id(1)

    def head_cols(h):
        return slice(h * HEAD_DIM, (h + 1) * HEAD_DIM)

    @pl.when(c == 0)
    def _():
        sfx_ref[...] = _suffix_matrix()
        bias = _causal_bias(rows, CHUNK)
        pad = jnp.zeros((CHUNK - rows, HEAD_DIM), BF16)
        for h in range(N_HEADS):
            q = q_ref[:, head_cols(h)]
            kn = jnp.concatenate([kn_ref[:, head_cols(h)].astype(BF16), pad], axis=0)
            vn = jnp.concatenate([vn_ref[:, head_cols(h)].astype(BF16), pad], axis=0)
            carry, w = _sb_weights(_scores(q, kn) + bias, jnp.zeros((rows, LANES), F32),
                                   sfx_ref[...], stack_chunks=True)
            carry_ref[h] = carry
            acc_ref[h] = jnp.dot(w, vn, preferred_element_type=F32)

    for h in range(N_HEADS):
        q = q_ref[:, head_cols(h)]
        k = kp_ref[pl.ds(h, DECODE_KEYS, stride=N_HEADS), :].astype(BF16)
        v = vp_ref[pl.ds(h, DECODE_KEYS, stride=N_HEADS), :].astype(BF16)
        carry, w = _sb_weights(_scores(q, k), carry_ref[h], sfx_ref[...], stack_chunks=True)
        carry_ref[h] = carry
        acc_ref[h] += jnp.dot(w, v, preferred_element_type=F32)

    @pl.when(c == pl.num_programs(1) - 1)
    def _():
        for h in range(N_HEADS):
            o_ref[:, head_cols(h)] = acc_ref[h].astype(BF16)


def _attn_decode(q, k_new, v_new, k_past, v_past, *, batch, rows, past):
    assert rows <= LANES and past % DECODE_KEYS == 0
    n_chunks = past // DECODE_KEYS
    new_spec = pl.BlockSpec((rows, ATTN_WIDTH), lambda b, c: (b, 0))
    past_spec = pl.BlockSpec((DECODE_KEYS * N_HEADS, HEAD_DIM),
                             lambda b, c: (b * n_chunks + n_chunks - 1 - c, 0))
    return pl.pallas_call(
        functools.partial(_attn_decode_kernel, rows=rows),
        grid=(batch, n_chunks),
        in_specs=[new_spec, new_spec, new_spec, past_spec, past_spec],
        out_specs=new_spec,
        out_shape=jax.ShapeDtypeStruct((batch * rows, ATTN_WIDTH), BF16),
        scratch_shapes=[pltpu.VMEM((CHUNK, CHUNK), BF16),
                        pltpu.VMEM((N_HEADS, rows, LANES), F32),
                        pltpu.VMEM((N_HEADS, rows, HEAD_DIM), F32)],
        compiler_params=_params("arbitrary", "arbitrary"),
        name="attn_decode",
    )(q, k_new, v_new, k_past, v_past)


def _mix_out_kernel(oa_ref, ob_ref, gates_ref, x_ref, wa_ref, wb_ref, wo_ref, h_ref, m_ref):
    chunks = [slice(c * MIX_COLS, (c + 1) * MIX_COLS) for c in range(D_MODEL // MIX_COLS)]
    for cols in chunks:
        gate_b_cols = slice(D_MODEL + cols.start, D_MODEL + cols.stop)
        ya = jnp.dot(oa_ref[...], wa_ref[:, cols], preferred_element_type=F32)
        yb = jnp.dot(ob_ref[...], wb_ref[:, cols], preferred_element_type=F32)
        m_ref[:, cols] = (_sigmoid(gates_ref[:, cols]) * ya
                          + _sigmoid(gates_ref[:, gate_b_cols]) * yb).astype(BF16)
    for cols in chunks:
        h_ref[:, cols] = x_ref[:, cols] + jnp.dot(m_ref[...], wo_ref[:, cols],
                                                  preferred_element_type=F32)


def _mix_out(o_a, o_b, gates, x, w_a, w_b, w_out, *, tm):
    n = x.shape[0]

    def rows(width):
        return pl.BlockSpec((tm, width), lambda i: (i, 0))

    def resident(w):
        return pl.BlockSpec(w.shape, lambda i: (0, 0), pipeline_mode=pl.Buffered(1))

    return pl.pallas_call(
        _mix_out_kernel,
        grid=(n // tm,),
        in_specs=[rows(POOL_WIDTH), rows(ATTN_WIDTH), rows(2 * D_MODEL), rows(D_MODEL),
                  resident(w_a), resident(w_b), resident(w_out)],
        out_specs=rows(D_MODEL),
        out_shape=jax.ShapeDtypeStruct((n, D_MODEL), F32),
        scratch_shapes=[pltpu.VMEM((tm, D_MODEL), BF16)],
        compiler_params=_params("arbitrary"),
        name="mix_out",
    )(o_a, o_b, gates, x, w_a, w_b, w_out)


def _ffn_up_kernel(h_ref, g_ref, wg_ref, wu_ref, hid_ref, n_ref):
    @pl.when(pl.program_id(1) == 0)
    def _():
        n_ref[...] = _rmsnorm(h_ref[...], g_ref[...]).astype(BF16)

    gate = jnp.dot(n_ref[...], wg_ref[...], preferred_element_type=F32)
    up = jnp.dot(n_ref[...], wu_ref[...], preferred_element_type=F32)
    hid_ref[...] = (gate * _sigmoid(gate) * up).astype(BF16)


def _ffn_up(h, g_ffn, w_gate_up, *, tm):
    n = h.shape[0]
    tn = WEIGHT_TILE
    nj = D_FF // tn
    return pl.pallas_call(
        _ffn_up_kernel,
        grid=(n // tm, nj),
        in_specs=[
            pl.BlockSpec((tm, D_MODEL), lambda i, j: (i, 0)),
            pl.BlockSpec((1, D_MODEL), lambda i, j: (0, 0)),
            pl.BlockSpec((D_MODEL, tn), lambda i, j: (0, j)),
            pl.BlockSpec((D_MODEL, tn), lambda i, j: (0, nj + j)),
        ],
        out_specs=pl.BlockSpec((tm, tn), lambda i, j: (i, j)),
        out_shape=jax.ShapeDtypeStruct((n, D_FF), BF16),
        scratch_shapes=[pltpu.VMEM((tm, D_MODEL), BF16)],
        compiler_params=_params("arbitrary", "arbitrary"),
        name="ffn_up",
    )(h, g_ffn, w_gate_up, w_gate_up)


def _ffn_down_kernel(hid_ref, w_ref, h_ref, g_ref, y_ref):
    k = pl.program_id(1)

    @pl.when(k == 0)
    def _():
        y_ref[...] = h_ref[...]

    y_ref[...] += jnp.dot(hid_ref[...], w_ref[...], preferred_element_type=F32)

    @pl.when(k == pl.num_programs(1) - 1)
    def _():
        y_ref[...] = _rmsnorm(y_ref[...], g_ref[...])


def _ffn_down(hid, w_down, h, g_final, *, tm, tk):
    n = h.shape[0]
    n_blocks = n // tm
    return pl.pallas_call(
        _ffn_down_kernel,
        grid=(n_blocks, D_FF // tk),
        in_specs=[
            pl.BlockSpec((tm, tk), lambda i, k: (i, k)),
            pl.BlockSpec((tk, D_MODEL), lambda i, k: (k, 0)),
            pl.BlockSpec((tm, D_MODEL),
                         lambda i, k: (jnp.where(k > 0, jnp.minimum(i + 1, n_blocks - 1), i), 0)),
            pl.BlockSpec((1, D_MODEL), lambda i, k: (0, 0)),
        ],
        out_specs=pl.BlockSpec((tm, D_MODEL), lambda i, k: (i, 0)),
        out_shape=jax.ShapeDtypeStruct((n, D_MODEL), F32),
        compiler_params=_params("arbitrary", "arbitrary"),
        name="ffn_down",
    )(hid, w_down, h, g_final)


def _layer(x, hist, past_kv, weights, g_final, *, batch, seq, pos0):
    g_mix, w_in, w_pool, s_pool, w_a, w_b, w_out, g_ffn, w_gate_up, w_down = weights
    n = batch * seq
    tm = min(n, ROW_BLOCK)
    u, q, k, v, gates = _in_proj(x, g_mix, w_in, tm=tm)
    o_a = _pool(u, hist, w_pool, s_pool, batch=batch, seq=seq, tm=min(seq, POOL_ROWS), pos0=pos0)
    if past_kv is None:
        o_b = _attn_prompt(q, k, v, batch=batch, seq=seq)
    else:
        o_b = _attn_decode(q, k, v, past_kv[0], past_kv[1], batch=batch, rows=seq, past=pos0)
    h = _mix_out(o_a, o_b, gates, x, w_a, w_b, w_out, tm=min(n, MIX_ROWS))
    hid = _ffn_up(h, g_ffn, w_gate_up, tm=tm)
    y = _ffn_down(hid, w_down, h, g_final, tm=tm, tk=WEIGHT_TILE)
    return y, u, k, v


def kernel(x_prompt, x_sample, cache_k, cache_v, state_pool, g_mix, w_in, w_pool, s_pool,
           w_branch, w_out, g_ffn, w_gate_up, w_down, g_final):
    depth = w_in.shape[0]
    assert depth == 1
    b_p, t_p, _ = x_prompt.shape
    b_s, t_s, _ = x_sample.shape
    past = cache_k.shape[2]
    l = 0
    weights = (
        g_mix[l][None, :], w_in[l].astype(BF16), w_pool[l].astype(BF16), s_pool[l][None, :],
        w_branch[l, :POOL_WIDTH].astype(BF16), w_branch[l, POOL_WIDTH:].astype(BF16),
        w_out[l].astype(BF16), g_ffn[l][None, :], w_gate_up[l].astype(BF16),
        w_down[l].astype(BF16),
    )
    g_fin = g_final[None, :]

    hist_p = jnp.zeros((b_p, HIST_ROWS, POOL_WIDTH), F32)
    y_p, u_p, k_p, v_p = _layer(
        x_prompt.reshape(b_p * t_p, D_MODEL), hist_p, None, weights, g_fin,
        batch=b_p, seq=t_p, pos0=0)

    hist_s = jnp.pad(state_pool[l], ((0, 0), (HIST_ROWS - POOL_HIST, 0), (0, 0)))
    past_kv = (cache_k[l].reshape(b_s * past * N_HEADS, HEAD_DIM),
               cache_v[l].reshape(b_s * past * N_HEADS, HEAD_DIM))
    y_s, u_s, k_s, v_s = _layer(
        x_sample.reshape(b_s * t_s, D_MODEL), hist_s, past_kv, weights, g_fin,
        batch=b_s, seq=t_s, pos0=past)

    new_pool_p = u_p.reshape(b_p, t_p, POOL_WIDTH)[:, t_p - POOL_HIST:]
    new_pool_s = jnp.concatenate([state_pool[l], u_s.reshape(b_s, t_s, POOL_WIDTH)],
                                 axis=1)[:, -POOL_HIST:]
    return (
        y_p.reshape(b_p, t_p, D_MODEL),
        y_s.reshape(b_s, t_s, D_MODEL),
        k_p.reshape(1, b_p, t_p, N_HEADS, HEAD_DIM),
        v_p.reshape(1, b_p, t_p, N_HEADS, HEAD_DIM),
        new_pool_p[None],
        k_s.reshape(1, b_s, t_s, N_HEADS, HEAD_DIM),
        v_s.reshape(1, b_s, t_s, N_HEADS, HEAD_DIM),
        new_pool_s[None],
    )
```

```python
import functools

import jax
import jax.numpy as jnp
from jax import lax
from jax.experimental import pallas as pl
from jax.experimental.pallas import tpu as pltpu

F32 = jnp.float32
BF16 = jnp.bfloat16

D_MODEL = 2048
N_HEADS = 8
HEAD_DIM = 128
ATTN_WIDTH = N_HEADS * HEAD_DIM
POOL_WINDOWS = (2, 4, 8, 16)
POOL_WIDTH = D_MODEL // 2
POOL_GROUP_WIDTH = POOL_WIDTH // len(POOL_WINDOWS)
POOL_HIST = max(POOL_WINDOWS) - 1
HIST_ROWS = POOL_HIST + 1
GROUP_WIDTH = 1024
N_GATE_GROUPS = 2 * D_MODEL // GROUP_WIDTH
D_FF = 5632
EPS = 1e-6

LANES = 128
CHUNK = 2 * LANES
DECODE_KEYS = 1024
Q_TILE = 512
Q_SCALE = 1.4426950408889634 * HEAD_DIM ** -0.5
VMEM_LIMIT = 56 * 1024 * 1024

ROW_BLOCK = 1024
WEIGHT_TILE = 512
MIX_ROWS = 512
MIX_COLS = 512
POOL_ROWS = 512


def _params(*sem):
    return pltpu.CompilerParams(dimension_semantics=sem, vmem_limit_bytes=VMEM_LIMIT)


def _rmsnorm(x, g):
    ms = jnp.mean(x * x, axis=-1, keepdims=True)
    return x * lax.rsqrt(ms + EPS) * g


def _sigmoid(x):
    return 1.0 / (1.0 + jnp.exp(-x))


def _inproj_kernel(x_ref, g_ref, w_lo_ref, w_hi_ref, u_ref, q_ref, k_ref, v_ref, gate_ref, xn_ref,
                   *, tpg):
    j = pl.program_id(1)
    half = D_MODEL // 2

    @pl.when(j == 0)
    def _():
        xn_ref[...] = _rmsnorm(x_ref[...], g_ref[...]).astype(BF16)

    def proj():
        return (jnp.dot(xn_ref[:, :half], w_lo_ref[...], preferred_element_type=F32)
                + jnp.dot(xn_ref[:, half:], w_hi_ref[...], preferred_element_type=F32))

    @pl.when(j < tpg)
    def _():
        u_ref[...] = proj()

    @pl.when((j >= tpg) & (j < 2 * tpg))
    def _():
        q_ref[...] = (proj() * Q_SCALE).astype(BF16)

    @pl.when((j >= 2 * tpg) & (j < 3 * tpg))
    def _():
        k_ref[...] = proj()

    @pl.when((j >= 3 * tpg) & (j < 4 * tpg))
    def _():
        v_ref[...] = proj()

    @pl.when(j >= 4 * tpg)
    def _():
        gate_ref[...] = proj()


def _in_proj(x, g_mix, w_in, *, tm):
    n = x.shape[0]
    tn = WEIGHT_TILE
    tpg = GROUP_WIDTH // tn
    n_tiles = w_in.shape[1] // tn
    n_blocks = n // tm

    def group_spec(first, ntiles):
        def index(i, j):
            moved_on = (j >= first + ntiles) & (i + 1 < n_blocks)
            return (jnp.where(moved_on, i + 1, i),
                    jnp.where(moved_on, 0, jnp.clip(j - first, 0, ntiles - 1)))

        return pl.BlockSpec((tm, tn), index)

    def group_shape(dtype):
        return jax.ShapeDtypeStruct((n, GROUP_WIDTH), dtype)

    return pl.pallas_call(
        functools.partial(_inproj_kernel, tpg=tpg),
        grid=(n_blocks, n_tiles),
        in_specs=[
            pl.BlockSpec((tm, D_MODEL),
                         lambda i, j: (jnp.where(j > 0, jnp.minimum(i + 1, n_blocks - 1), i), 0)),
            pl.BlockSpec((1, D_MODEL), lambda i, j: (0, 0)),
            pl.BlockSpec((D_MODEL // 2, tn), lambda i, j: (0, j)),
            pl.BlockSpec((D_MODEL // 2, tn), lambda i, j: (1, j)),
        ],
        out_specs=[group_spec(0, tpg), group_spec(tpg, tpg), group_spec(2 * tpg, tpg),
                   group_spec(3 * tpg, tpg), group_spec(4 * tpg, N_GATE_GROUPS * tpg)],
        out_shape=[group_shape(F32), group_shape(BF16), group_shape(F32), group_shape(F32),
                   jax.ShapeDtypeStruct((n, N_GATE_GROUPS * GROUP_WIDTH), F32)],
        scratch_shapes=[pltpu.VMEM((tm, D_MODEL), BF16)],
        compiler_params=_params("arbitrary", "arbitrary"),
        name="in_proj",
    )(x, g_mix, w_in, w_in)


def _pool_kernel(u_ref, uprev_ref, hist_ref, wp_ref, sp_ref, o_ref, buf_ref, *, tm, pos0):
    i = pl.program_id(1)
    buf_ref[HIST_ROWS:, :] = u_ref[...]

    @pl.when(i == 0)
    def _():
        buf_ref[:HIST_ROWS, :] = hist_ref[0]

    @pl.when(i > 0)
    def _():
        buf_ref[:HIST_ROWS, :] = uprev_ref[...]

    pos = pos0 + i * tm + lax.broadcasted_iota(jnp.int32, (tm, 1), 0)
    for g, w in enumerate(POOL_WINDOWS):
        cols = slice(g * POOL_GROUP_WIDTH, (g + 1) * POOL_GROUP_WIDTH)
        cur = buf_ref[HIST_ROWS:, cols]
        s = cur
        for d in range(1, w):
            s = s + buf_ref[HIST_ROWS - d:HIST_ROWS - d + tm, cols]
        cnt = jnp.minimum(pos + 1, w).astype(F32)
        diff = s / cnt - cur
        o = jnp.dot(diff.astype(BF16), wp_ref[g], preferred_element_type=F32)
        o_ref[:, cols] = (o * sp_ref[:, cols]).astype(BF16)


def _pool(u, hist, w_pool, s_pool, *, batch, seq, tm, pos0):
    nt = seq // tm
    per = tm // HIST_ROWS
    return pl.pallas_call(
        functools.partial(_pool_kernel, tm=tm, pos0=pos0),
        grid=(batch, nt),
        in_specs=[
            pl.BlockSpec((tm, POOL_WIDTH), lambda b, i: (b * nt + i, 0)),
            pl.BlockSpec((HIST_ROWS, POOL_WIDTH),
                         lambda b, i: (jnp.maximum((b * nt + i) * per - 1, 0), 0)),
            pl.BlockSpec((1, HIST_ROWS, POOL_WIDTH), lambda b, i: (b, 0, 0)),
            pl.BlockSpec(w_pool.shape, lambda b, i: (0, 0, 0)),
            pl.BlockSpec((1, POOL_WIDTH), lambda b, i: (0, 0)),
        ],
        out_specs=pl.BlockSpec((tm, POOL_WIDTH), lambda b, i: (b * nt + i, 0)),
        out_shape=jax.ShapeDtypeStruct((batch * seq, POOL_WIDTH), BF16),
        scratch_shapes=[pltpu.VMEM((HIST_ROWS + tm, POOL_WIDTH), F32)],
        compiler_params=_params("arbitrary", "arbitrary"),
        name="pool",
    )(u, u, hist, w_pool, s_pool)


def _suffix_matrix():
    r = lax.broadcasted_iota(jnp.int32, (CHUNK, CHUNK), 0)
    c = lax.broadcasted_iota(jnp.int32, (CHUNK, CHUNK), 1)
    return jnp.where(r >= c, 1.0, 0.0).astype(BF16)


def _scores(q, kblk):
    return lax.dot_general(q, kblk, (((1,), (1,)), ((), ())), preferred_element_type=F32)


def _sb_weights(z, carry, sfx, *, stack_chunks):
    rows, span = z.shape
    n_chunks = span // CHUNK
    sp = (jnp.maximum(z, 0.0) + jnp.log2(1.0 + jnp.exp2(-jnp.abs(z)))).astype(BF16)

    def chunk(x, c):
        return x[:, c * CHUNK:(c + 1) * CHUNK]

    if stack_chunks:
        r_all = jnp.dot(jnp.concatenate([chunk(sp, c) for c in range(n_chunks)], axis=0), sfx,
                        preferred_element_type=F32)
    ws = []
    for c in reversed(range(n_chunks)):
        if stack_chunks:
            r = r_all[c * rows:(c + 1) * rows]
        else:
            r = jnp.dot(chunk(sp, c), sfx, preferred_element_type=F32)
        w = jnp.exp2(chunk(z, c) - (r + jnp.concatenate([carry] * (CHUNK // LANES), axis=1)))
        ws.insert(0, w.astype(BF16))
        carry = carry + jnp.broadcast_to(r[:, :1], (rows, LANES))
    return carry, jnp.concatenate(ws, axis=1)


def _causal_bias(rows, span):
    r = lax.broadcasted_iota(jnp.int32, (rows, span), 0)
    c = lax.broadcasted_iota(jnp.int32, (rows, span), 1)
    return jnp.where(c < r, 0.0, MASKED_LOGIT)


ATTN_UNROLL = 2
MASKED_LOGIT = -1e9


def _attn_prompt_kernel(q_ref, k_ref, v_ref, o_ref, kb_ref, vt_ref, sfx_ref, bias_ref,
                        z_ref, w_ref, carry_ref, acct_ref, *, seq):
    tile = Q_TILE
    nq = seq // tile
    kb_ref[...] = k_ref[...].astype(BF16)
    for j in range(nq):
        vt_ref[j] = v_ref[j * tile:(j + 1) * tile, :].T.astype(BF16)

    @pl.when((pl.program_id(0) == 0) & (pl.program_id(1) == 0))
    def _():
        sfx_ref[...] = _suffix_matrix()
        bias_ref[...] = _causal_bias(tile, tile)

    def rows(ref, i):
        return ref[pl.ds(pl.multiple_of(i * tile, tile), tile), :]

    def logits(t, slot):
        qi, j = jnp.minimum(t[0], nq - 1), jnp.minimum(t[1], nq - 1)
        z_ref[slot] = _scores(rows(q_ref, qi), rows(kb_ref, j))

    def weights(t, slot, diagonal):
        qi, _ = t
        if diagonal:
            z, carry = z_ref[slot] + bias_ref[...], jnp.zeros((tile, LANES), F32)
        else:
            z, carry = z_ref[slot], carry_ref[qi]
        carry, w = _sb_weights(z, carry, sfx_ref[...], stack_chunks=False)
        carry_ref[qi] = carry
        w_ref[slot] = w

    def value_product(t, slot, store):
        qi, j = t
        pv = lax.dot_general(vt_ref[j], w_ref[slot], (((1,), (1,)), ((), ())),
                             preferred_element_type=F32)
        acct_ref[qi] = pv if store else acct_ref[qi] + pv

    def walk(n_steps, following, diagonal, st):
        assert n_steps % ATTN_UNROLL == 0 and ATTN_UNROLL % 2 == 0

        def group(i, st):
            prev, cur = (st[0], st[1]), (st[2], st[3])
            for s in range(ATTN_UNROLL):
                slot = s % 2
                nxt = following(cur)
                logits(nxt, 1 - slot)
                weights(cur, slot, diagonal)
                value_product(prev, 1 - slot, store=diagonal)
                prev, cur = cur, nxt
            return prev + cur

        return lax.fori_loop(0, n_steps // ATTN_UNROLL, group, st)

    def next_diagonal(t):
        last = t[0] == nq - 1
        return jnp.where(last, 1, t[0] + 1), jnp.where(last, 0, t[1] + 1)

    def next_below(t):
        qi, j = t
        wrap = j == 0
        return jnp.where(wrap, qi + 1, qi), jnp.where(wrap, qi, j - 1)

    zero = jnp.int32(0)
    acct_ref[nq - 1] = jnp.zeros((HEAD_DIM, tile), F32)
    w_ref[1] = jnp.zeros((tile, tile), BF16)
    logits((zero, zero), 0)
    st = walk(nq, next_diagonal, True, (zero, zero, zero, zero))
    st = walk(nq * (nq - 1) // 2, next_below, False, st)
    value_product((st[0], st[1]), 1, store=False)
    for qi in range(nq):
        o_ref[qi * tile:(qi + 1) * tile, :] = acct_ref[qi].T.astype(BF16)


def _attn_prompt(q, k, v, *, batch, seq):
    assert seq % Q_TILE == 0
    spec = pl.BlockSpec((seq, HEAD_DIM), lambda b, h: (b, h))
    return pl.pallas_call(
        functools.partial(_attn_prompt_kernel, seq=seq),
        grid=(batch, N_HEADS),
        in_specs=[spec, spec, spec],
        out_specs=spec,
        out_shape=jax.ShapeDtypeStruct((batch * seq, ATTN_WIDTH), BF16),
        scratch_shapes=[
            pltpu.VMEM((seq, HEAD_DIM), BF16),
            pltpu.VMEM((seq // Q_TILE, HEAD_DIM, Q_TILE), BF16),
            pltpu.VMEM((CHUNK, CHUNK), BF16),
            pltpu.VMEM((Q_TILE, Q_TILE), F32),
            pltpu.VMEM((2, Q_TILE, Q_TILE), F32),
            pltpu.VMEM((2, Q_TILE, Q_TILE), BF16),
            pltpu.VMEM((seq // Q_TILE, Q_TILE, LANES), F32),
            pltpu.VMEM((seq // Q_TILE, HEAD_DIM, Q_TILE), F32),
        ],
        compiler_params=_params("arbitrary", "arbitrary"),
        name="attn_prompt",
    )(q, k, v)


def _attn_decode_kernel(q_ref, kn_ref, vn_ref, kp_ref, vp_ref, o_ref, sfx_ref, carry_ref, acc_ref,
                        *, rows):
    c = pl.program_id(1)

    def head_cols(h):
        return slice(h * HEAD_DIM, (h + 1) * HEAD_DIM)

    @pl.when(c == 0)
    def _():
        sfx_ref[...] = _suffix_matrix()
        bias = _causal_bias(rows, CHUNK)
        pad = jnp.zeros((CHUNK - rows, HEAD_DIM), BF16)
        for h in range(N_HEADS):
            q = q_ref[:, head_cols(h)]
            kn = jnp.concatenate([kn_ref[:, head_cols(h)].astype(BF16), pad], axis=0)
            vn = jnp.concatenate([vn_ref[:, head_cols(h)].astype(BF16), pad], axis=0)
            carry, w = _sb_weights(_scores(q, kn) + bias, jnp.zeros((rows, LANES), F32),
                                   sfx_ref[...], stack_chunks=True)
            carry_ref[h] = carry
            acc_ref[h] = jnp.dot(w, vn, preferred_element_type=F32)

    for h in range(N_HEADS):
        q = q_ref[:, head_cols(h)]
        k = kp_ref[pl.ds(h, DECODE_KEYS, stride=N_HEADS), :].astype(BF16)
        v = vp_ref[pl.ds(h, DECODE_KEYS, stride=N_HEADS), :].astype(BF16)
        carry, w = _sb_weights(_scores(q, k), carry_ref[h], sfx_ref[...], stack_chunks=True)
        carry_ref[h] = carry
        acc_ref[h] += jnp.dot(w, v, preferred_element_type=F32)

    @pl.when(c == pl.num_programs(1) - 1)
    def _():
        for h in range(N_HEADS):
            o_ref[:, head_cols(h)] = acc_ref[h].astype(BF16)


def _attn_decode(q, k_new, v_new, k_past, v_past, *, batch, rows, past):
    assert rows <= LANES and past % DECODE_KEYS == 0
    n_chunks = past // DECODE_KEYS
    new_spec = pl.BlockSpec((rows, ATTN_WIDTH), lambda b, c: (b, 0))
    past_spec = pl.BlockSpec((DECODE_KEYS * N_HEADS, HEAD_DIM),
                             lambda b, c: (b * n_chunks + n_chunks - 1 - c, 0))
    return pl.pallas_call(
        functools.partial(_attn_decode_kernel, rows=rows),
        grid=(batch, n_chunks),
        in_specs=[new_spec, new_spec, new_spec, past_spec, past_spec],
        out_specs=new_spec,
        out_shape=jax.ShapeDtypeStruct((batch * rows, ATTN_WIDTH), BF16),
        scratch_shapes=[pltpu.VMEM((CHUNK, CHUNK), BF16),
                        pltpu.VMEM((N_HEADS, rows, LANES), F32),
                        pltpu.VMEM((N_HEADS, rows, HEAD_DIM), F32)],
        compiler_params=_params("arbitrary", "arbitrary"),
        name="attn_decode",
    )(q, k_new, v_new, k_past, v_past)


def _mix_out_kernel(oa_ref, ob_ref, ga_ref, gb_ref, x_ref, wa_ref, wb_ref, wo_ref, h_ref, m_ref):
    chunks = [slice(c * MIX_COLS, (c + 1) * MIX_COLS) for c in range(D_MODEL // MIX_COLS)]
    for cols in chunks:
        ya = jnp.dot(oa_ref[...], wa_ref[:, cols], preferred_element_type=F32)
        yb = jnp.dot(ob_ref[...], wb_ref[:, cols], preferred_element_type=F32)
        m_ref[:, cols] = (_sigmoid(ga_ref[:, cols]) * ya
                          + _sigmoid(gb_ref[:, cols]) * yb).astype(BF16)
    for cols in chunks:
        h_ref[:, cols] = x_ref[:, cols] + jnp.dot(m_ref[...], wo_ref[:, cols],
                                                  preferred_element_type=F32)


def _mix_out(o_a, o_b, gates, x, w_a, w_b, w_out, *, tm):
    n = x.shape[0]

    def rows(width):
        return pl.BlockSpec((tm, width), lambda i: (i, 0))

    def resident(w):
        return pl.BlockSpec(w.shape, lambda i: (0, 0), pipeline_mode=pl.Buffered(1))

    return pl.pallas_call(
        _mix_out_kernel,
        grid=(n // tm,),
        in_specs=[rows(POOL_WIDTH), rows(ATTN_WIDTH), rows(D_MODEL),
                  pl.BlockSpec((tm, D_MODEL), lambda i: (i, 1)), rows(D_MODEL),
                  resident(w_a), resident(w_b), resident(w_out)],
        out_specs=rows(D_MODEL),
        out_shape=jax.ShapeDtypeStruct((n, D_MODEL), F32),
        scratch_shapes=[pltpu.VMEM((tm, D_MODEL), BF16)],
        compiler_params=_params("arbitrary"),
        name="mix_out",
    )(o_a, o_b, gates, gates, x, w_a, w_b, w_out)


def _ffn_up_kernel(h_ref, g_ref, wg_ref, wu_ref, hid_ref, n_ref):
    @pl.when(pl.program_id(1) == 0)
    def _():
        n_ref[...] = _rmsnorm(h_ref[...], g_ref[...]).astype(BF16)

    gate = jnp.dot(n_ref[...], wg_ref[...], preferred_element_type=F32)
    up = jnp.dot(n_ref[...], wu_ref[...], preferred_element_type=F32)
    hid_ref[...] = (gate * _sigmoid(gate) * up).astype(BF16)


def _ffn_up(h, g_ffn, w_gate_up, *, tm):
    n = h.shape[0]
    tn = WEIGHT_TILE
    nj = D_FF // tn
    return pl.pallas_call(
        _ffn_up_kernel,
        grid=(n // tm, nj),
        in_specs=[
            pl.BlockSpec((tm, D_MODEL), lambda i, j: (i, 0)),
            pl.BlockSpec((1, D_MODEL), lambda i, j: (0, 0)),
            pl.BlockSpec((D_MODEL, tn), lambda i, j: (0, j)),
            pl.BlockSpec((D_MODEL, tn), lambda i, j: (0, nj + j)),
        ],
        out_specs=pl.BlockSpec((tm, tn), lambda i, j: (i, j)),
        out_shape=jax.ShapeDtypeStruct((n, D_FF), BF16),
        scratch_shapes=[pltpu.VMEM((tm, D_MODEL), BF16)],
        compiler_params=_params("arbitrary", "arbitrary"),
        name="ffn_up",
    )(h, g_ffn, w_gate_up, w_gate_up)


def _ffn_down_kernel(hid_ref, w_ref, h_ref, g_ref, y_ref):
    k = pl.program_id(1)

    @pl.when(k == 0)
    def _():
        y_ref[...] = h_ref[...]

    y_ref[...] += jnp.dot(hid_ref[...], w_ref[...], preferred_element_type=F32)

    @pl.when(k == pl.num_programs(1) - 1)
    def _():
        y_ref[...] = _rmsnorm(y_ref[...], g_ref[...])


def _ffn_down(hid, w_down, h, g_final, *, tm, tk):
    n = h.shape[0]
    n_blocks = n // tm
    n_k = D_FF // tk
    return pl.pallas_call(
        _ffn_down_kernel,
        grid=(n_blocks, n_k),
        in_specs=[
            pl.BlockSpec((tm, tk), lambda i, k: (i, k)),
            pl.BlockSpec((tk, D_MODEL), lambda i, k: (k, 0)),
            pl.BlockSpec((tm, D_MODEL),
                         lambda i, k: (jnp.where(k > n_k // 2, jnp.minimum(i + 1, n_blocks - 1), i), 0)),
            pl.BlockSpec((1, D_MODEL), lambda i, k: (0, 0)),
        ],
        out_specs=pl.BlockSpec((tm, D_MODEL), lambda i, k: (i, 0)),
        out_shape=jax.ShapeDtypeStruct((n, D_MODEL), F32),
        compiler_params=_params("arbitrary", "arbitrary"),
        name="ffn_down",
    )(hid, w_down, h, g_final)


def _layer(x, hist, past_kv, weights, g_final, *, batch, seq, pos0):
    g_mix, w_in, w_pool, s_pool, w_a, w_b, w_out, g_ffn, w_gate_up, w_down = weights
    n = batch * seq
    tm = min(n, ROW_BLOCK)
    u, q, k, v, gates = _in_proj(x, g_mix, w_in, tm=tm)
    o_a = _pool(u, hist, w_pool, s_pool, batch=batch, seq=seq, tm=min(seq, POOL_ROWS), pos0=pos0)
    if past_kv is None:
        o_b = _attn_prompt(q, k, v, batch=batch, seq=seq)
    else:
        o_b = _attn_decode(q, k, v, past_kv[0], past_kv[1], batch=batch, rows=seq, past=pos0)
    h = _mix_out(o_a, o_b, gates, x, w_a, w_b, w_out, tm=min(n, MIX_ROWS))
    hid = _ffn_up(h, g_ffn, w_gate_up, tm=tm)
    y = _ffn_down(hid, w_down, h, g_final, tm=tm, tk=WEIGHT_TILE)
    return y, u, k, v


def kernel(x_prompt, x_sample, cache_k, cache_v, state_pool, g_mix, w_in, w_pool, s_pool,
           w_branch, w_out, g_ffn, w_gate_up, w_down, g_final):
    depth = w_in.shape[0]
    assert depth == 1
    b_p, t_p, _ = x_prompt.shape
    b_s, t_s, _ = x_sample.shape
    past = cache_k.shape[2]
    l = 0
    weights = (
        g_mix[l][None, :], w_in[l].astype(BF16), w_pool[l].astype(BF16), s_pool[l][None, :],
        w_branch[l, :POOL_WIDTH].astype(BF16), w_branch[l, POOL_WIDTH:].astype(BF16),
        w_out[l].astype(BF16), g_ffn[l][None, :], w_gate_up[l].astype(BF16),
        w_down[l].astype(BF16),
    )
    g_fin = g_final[None, :]

    hist_p = jnp.zeros((b_p, HIST_ROWS, POOL_WIDTH), F32)
    y_p, u_p, k_p, v_p = _layer(
        x_prompt.reshape(b_p * t_p, D_MODEL), hist_p, None, weights, g_fin,
        batch=b_p, seq=t_p, pos0=0)

    hist_s = jnp.pad(state_pool[l], ((0, 0), (HIST_ROWS - POOL_HIST, 0), (0, 0)))
    past_kv = (cache_k[l].reshape(b_s * past * N_HEADS, HEAD_DIM),
               cache_v[l].reshape(b_s * past * N_HEADS, HEAD_DIM))
    y_s, u_s, k_s, v_s = _layer(
        x_sample.reshape(b_s * t_s, D_MODEL), hist_s, past_kv, weights, g_fin,
        batch=b_s, seq=t_s, pos0=past)

    new_pool_p = u_p.reshape(b_p, t_p, POOL_WIDTH)[:, t_p - POOL_HIST:]
    new_pool_s = jnp.concatenate([state_pool[l], u_s.reshape(b_s, t_s, POOL_WIDTH)],
                                 axis=1)[:, -POOL_HIST:]
    return (
        y_p.reshape(b_p, t_p, D_MODEL),
        y_s.reshape(b_s, t_s, D_MODEL),
        k_p.reshape(1, b_p, t_p, N_HEADS, HEAD_DIM),
        v_p.reshape(1, b_p, t_p, N_HEADS, HEAD_DIM),
        new_pool_p[None],
        k_s.reshape(1, b_s, t_s, N_HEADS, HEAD_DIM),
        v_s.reshape(1, b_s, t_s, N_HEADS, HEAD_DIM),
        new_pool_s[None],
    )
```

```python
import functools

import jax
import jax.numpy as jnp
from jax import lax
from jax.experimental import pallas as pl
from jax.experimental.pallas import tpu as pltpu

F32 = jnp.float32
BF16 = jnp.bfloat16

D_MODEL = 2048
N_HEADS = 8
HEAD_DIM = 128
ATTN_WIDTH = N_HEADS * HEAD_DIM
POOL_WINDOWS = (2, 4, 8, 16)
POOL_WIDTH = D_MODEL // 2
POOL_GROUP_WIDTH = POOL_WIDTH // len(POOL_WINDOWS)
POOL_HIST = max(POOL_WINDOWS) - 1
HIST_ROWS = POOL_HIST + 1
GROUP_WIDTH = 1024
N_GATE_GROUPS = 2 * D_MODEL // GROUP_WIDTH
D_FF = 5632
EPS = 1e-6

LANES = 128
CHUNK = 2 * LANES
DECODE_KEYS = 1024
Q_TILE = 512
Q_SCALE = 1.4426950408889634 * HEAD_DIM ** -0.5
VMEM_LIMIT = 56 * 1024 * 1024
FFN_VMEM_LIMIT = 60 * 1024 * 1024

ROW_BLOCK = 1024
WEIGHT_TILE = 512
MIX_ROWS = 512
MIX_COLS = 512
POOL_ROWS = 512
NORM_ROWS = 256


def _params(*sem, vmem_limit=VMEM_LIMIT):
    return pltpu.CompilerParams(dimension_semantics=sem, vmem_limit_bytes=vmem_limit)


def _rmsnorm(x, g):
    ms = jnp.mean(x * x, axis=-1, keepdims=True)
    return x * lax.rsqrt(ms + EPS) * g


def _sigmoid(x):
    return 1.0 / (1.0 + jnp.exp(-x))


def _inproj_kernel(x_ref, g_ref, w_ref, u_ref, q_ref, k_ref, v_ref, gate_ref, xn_ref, *, tpg):
    j = pl.program_id(1)

    @pl.when(j == 0)
    def _():
        xn_ref[...] = _rmsnorm(x_ref[...], g_ref[...]).astype(BF16)

    def proj():
        return jnp.dot(xn_ref[...], w_ref[...], preferred_element_type=F32)

    @pl.when(j < tpg)
    def _():
        u_ref[...] = proj()

    @pl.when((j >= tpg) & (j < 2 * tpg))
    def _():
        q_ref[...] = (proj() * Q_SCALE).astype(BF16)

    @pl.when((j >= 2 * tpg) & (j < 3 * tpg))
    def _():
        k_ref[...] = proj()

    @pl.when((j >= 3 * tpg) & (j < 4 * tpg))
    def _():
        v_ref[...] = proj()

    @pl.when(j >= 4 * tpg)
    def _():
        gate_ref[...] = proj()


def _in_proj(x, g_mix, w_in, *, tm):
    n = x.shape[0]
    tn = WEIGHT_TILE
    tpg = GROUP_WIDTH // tn
    n_tiles = w_in.shape[1] // tn
    n_blocks = n // tm

    def group_spec(first, ntiles):
        def index(i, j):
            moved_on = (j >= first + ntiles) & (i + 1 < n_blocks)
            return (jnp.where(moved_on, i + 1, i),
                    jnp.where(moved_on, 0, jnp.clip(j - first, 0, ntiles - 1)))

        return pl.BlockSpec((tm, tn), index)

    def group_shape(dtype):
        return jax.ShapeDtypeStruct((n, GROUP_WIDTH), dtype)

    return pl.pallas_call(
        functools.partial(_inproj_kernel, tpg=tpg),
        grid=(n_blocks, n_tiles),
        in_specs=[
            pl.BlockSpec((tm, D_MODEL),
                         lambda i, j: (jnp.where(j > 0, jnp.minimum(i + 1, n_blocks - 1), i), 0)),
            pl.BlockSpec((1, D_MODEL), lambda i, j: (0, 0)),
            pl.BlockSpec((D_MODEL, tn), lambda i, j: (0, j)),
        ],
        out_specs=[group_spec(0, tpg), group_spec(tpg, tpg), group_spec(2 * tpg, tpg),
                   group_spec(3 * tpg, tpg), group_spec(4 * tpg, N_GATE_GROUPS * tpg)],
        out_shape=[group_shape(F32), group_shape(BF16), group_shape(F32), group_shape(F32),
                   jax.ShapeDtypeStruct((n, N_GATE_GROUPS * GROUP_WIDTH), F32)],
        scratch_shapes=[pltpu.VMEM((tm, D_MODEL), BF16)],
        compiler_params=_params("arbitrary", "arbitrary"),
        name="in_proj",
    )(x, g_mix, w_in)


def _pool_kernel(u_ref, uprev_ref, hist_ref, wp_ref, sp_ref, o_ref, buf_ref, *, tm, pos0):
    i = pl.program_id(1)
    buf_ref[HIST_ROWS:, :] = u_ref[...]

    @pl.when(i == 0)
    def _():
        buf_ref[:HIST_ROWS, :] = hist_ref[0]

    @pl.when(i > 0)
    def _():
        buf_ref[:HIST_ROWS, :] = uprev_ref[...]

    pos = pos0 + i * tm + lax.broadcasted_iota(jnp.int32, (tm, 1), 0)
    for g, w in enumerate(POOL_WINDOWS):
        cols = slice(g * POOL_GROUP_WIDTH, (g + 1) * POOL_GROUP_WIDTH)
        cur = buf_ref[HIST_ROWS:, cols]
        s = cur
        for d in range(1, w):
            s = s + buf_ref[HIST_ROWS - d:HIST_ROWS - d + tm, cols]
        cnt = jnp.minimum(pos + 1, w).astype(F32)
        diff = s / cnt - cur
        o = jnp.dot(diff.astype(BF16), wp_ref[g], preferred_element_type=F32)
        o_ref[:, cols] = (o * sp_ref[:, cols]).astype(BF16)


def _pool(u, hist, w_pool, s_pool, *, batch, seq, tm, pos0):
    nt = seq // tm
    per = tm // HIST_ROWS
    return pl.pallas_call(
        functools.partial(_pool_kernel, tm=tm, pos0=pos0),
        grid=(batch, nt),
        in_specs=[
            pl.BlockSpec((tm, POOL_WIDTH), lambda b, i: (b * nt + i, 0)),
            pl.BlockSpec((HIST_ROWS, POOL_WIDTH),
                         lambda b, i: (jnp.maximum((b * nt + i) * per - 1, 0), 0)),
            pl.BlockSpec((1, HIST_ROWS, POOL_WIDTH), lambda b, i: (b, 0, 0)),
            pl.BlockSpec(w_pool.shape, lambda b, i: (0, 0, 0)),
            pl.BlockSpec((1, POOL_WIDTH), lambda b, i: (0, 0)),
        ],
        out_specs=pl.BlockSpec((tm, POOL_WIDTH), lambda b, i: (b * nt + i, 0)),
        out_shape=jax.ShapeDtypeStruct((batch * seq, POOL_WIDTH), BF16),
        scratch_shapes=[pltpu.VMEM((HIST_ROWS + tm, POOL_WIDTH), F32)],
        compiler_params=_params("arbitrary", "arbitrary"),
        name="pool",
    )(u, u, hist, w_pool, s_pool)


def _suffix_matrix():
    r = lax.broadcasted_iota(jnp.int32, (CHUNK, CHUNK), 0)
    c = lax.broadcasted_iota(jnp.int32, (CHUNK, CHUNK), 1)
    return jnp.where(r >= c, 1.0, 0.0).astype(BF16)


def _scores(q, kblk):
    return lax.dot_general(q, kblk, (((1,), (1,)), ((), ())), preferred_element_type=F32)


def _sb_weights(z, carry, sfx, *, stack_chunks):
    rows, span = z.shape
    n_chunks = span // CHUNK
    sp = (jnp.maximum(z, 0.0) + jnp.log2(1.0 + jnp.exp2(-jnp.abs(z)))).astype(BF16)

    def chunk(x, c):
        return x[:, c * CHUNK:(c + 1) * CHUNK]

    if stack_chunks:
        r_all = jnp.dot(jnp.concatenate([chunk(sp, c) for c in range(n_chunks)], axis=0), sfx,
                        preferred_element_type=F32)
    ws = []
    for c in reversed(range(n_chunks)):
        if stack_chunks:
            r = r_all[c * rows:(c + 1) * rows]
        else:
            r = jnp.dot(chunk(sp, c), sfx, preferred_element_type=F32)
        w = jnp.exp2(chunk(z, c) - (r + jnp.concatenate([carry] * (CHUNK // LANES), axis=1)))
        ws.insert(0, w.astype(BF16))
        carry = carry + jnp.broadcast_to(r[:, :1], (rows, LANES))
    return carry, jnp.concatenate(ws, axis=1)


def _causal_bias(rows, span):
    r = lax.broadcasted_iota(jnp.int32, (rows, span), 0)
    c = lax.broadcasted_iota(jnp.int32, (rows, span), 1)
    return jnp.where(c < r, 0.0, MASKED_LOGIT)


ATTN_UNROLL = 2
MASKED_LOGIT = -1e9


def _attn_prompt_kernel(q_ref, k_ref, v_ref, o_ref, kb_ref, vt_ref, sfx_ref, bias_ref,
                        z_ref, w_ref, carry_ref, acct_ref, *, seq):
    tile = Q_TILE
    nq = seq // tile
    kb_ref[...] = k_ref[...].astype(BF16)
    for j in range(nq):
        vt_ref[j] = v_ref[j * tile:(j + 1) * tile, :].T.astype(BF16)

    @pl.when((pl.program_id(0) == 0) & (pl.program_id(1) == 0))
    def _():
        sfx_ref[...] = _suffix_matrix()
        bias_ref[...] = _causal_bias(tile, tile)

    def rows(ref, i):
        return ref[pl.ds(pl.multiple_of(i * tile, tile), tile), :]

    def logits(t, slot):
        qi, j = jnp.minimum(t[0], nq - 1), jnp.minimum(t[1], nq - 1)
        z_ref[slot] = _scores(rows(q_ref, qi), rows(kb_ref, j))

    def weights(t, slot, diagonal):
        qi, _ = t
        if diagonal:
            z, carry = z_ref[slot] + bias_ref[...], jnp.zeros((tile, LANES), F32)
        else:
            z, carry = z_ref[slot], carry_ref[qi]
        carry, w = _sb_weights(z, carry, sfx_ref[...], stack_chunks=False)
        carry_ref[qi] = carry
        w_ref[slot] = w

    def value_product(t, slot, store):
        qi, j = t
        pv = lax.dot_general(vt_ref[j], w_ref[slot], (((1,), (1,)), ((), ())),
                             preferred_element_type=F32)
        acct_ref[qi] = pv if store else acct_ref[qi] + pv

    def walk(n_steps, following, diagonal, st):
        assert n_steps % ATTN_UNROLL == 0 and ATTN_UNROLL % 2 == 0

        def group(i, st):
            prev, cur = (st[0], st[1]), (st[2], st[3])
            for s in range(ATTN_UNROLL):
                slot = s % 2
                nxt = following(cur)
                logits(nxt, 1 - slot)
                weights(cur, slot, diagonal)
                value_product(prev, 1 - slot, store=diagonal)
                prev, cur = cur, nxt
            return prev + cur

        return lax.fori_loop(0, n_steps // ATTN_UNROLL, group, st)

    def next_diagonal(t):
        last = t[0] == nq - 1
        return jnp.where(last, 1, t[0] + 1), jnp.where(last, 0, t[1] + 1)

    def next_below(t):
        qi, j = t
        wrap = j == 0
        return jnp.where(wrap, qi + 1, qi), jnp.where(wrap, qi, j - 1)

    zero = jnp.int32(0)
    acct_ref[nq - 1] = jnp.zeros((HEAD_DIM, tile), F32)
    w_ref[1] = jnp.zeros((tile, tile), BF16)
    logits((zero, zero), 0)
    st = walk(nq, next_diagonal, True, (zero, zero, zero, zero))
    st = walk(nq * (nq - 1) // 2, next_below, False, st)
    value_product((st[0], st[1]), 1, store=False)
    for qi in range(nq):
        o_ref[qi * tile:(qi + 1) * tile, :] = acct_ref[qi].T.astype(BF16)


def _attn_prompt(q, k, v, *, batch, seq):
    assert seq % Q_TILE == 0
    spec = pl.BlockSpec((seq, HEAD_DIM), lambda b, h: (b, h))
    return pl.pallas_call(
        functools.partial(_attn_prompt_kernel, seq=seq),
        grid=(batch, N_HEADS),
        in_specs=[spec, spec, spec],
        out_specs=spec,
        out_shape=jax.ShapeDtypeStruct((batch * seq, ATTN_WIDTH), BF16),
        scratch_shapes=[
            pltpu.VMEM((seq, HEAD_DIM), BF16),
            pltpu.VMEM((seq // Q_TILE, HEAD_DIM, Q_TILE), BF16),
            pltpu.VMEM((CHUNK, CHUNK), BF16),
            pltpu.VMEM((Q_TILE, Q_TILE), F32),
            pltpu.VMEM((2, Q_TILE, Q_TILE), F32),
            pltpu.VMEM((2, Q_TILE, Q_TILE), BF16),
            pltpu.VMEM((seq // Q_TILE, Q_TILE, LANES), F32),
            pltpu.VMEM((seq // Q_TILE, HEAD_DIM, Q_TILE), F32),
        ],
        compiler_params=_params("arbitrary", "arbitrary"),
        name="attn_prompt",
    )(q, k, v)


def _attn_decode_kernel(q_ref, kn_ref, vn_ref, kp_ref, vp_ref, o_ref, sfx_ref, carry_ref, acc_ref,
                        *, rows):
    c = pl.program_id(1)

    def head_cols(h):
        return slice(h * HEAD_DIM, (h + 1) * HEAD_DIM)

    @pl.when(c == 0)
    def _():
        sfx_ref[...] = _suffix_matrix()
        bias = _causal_bias(rows, CHUNK)
        pad = jnp.zeros((CHUNK - rows, HEAD_DIM), BF16)
        for h in range(N_HEADS):
            q = q_ref[:, head_cols(h)]
            kn = jnp.concatenate([kn_ref[:, head_cols(h)].astype(BF16), pad], axis=0)
            vn = jnp.concatenate([vn_ref[:, head_cols(h)].astype(BF16), pad], axis=0)
            carry, w = _sb_weights(_scores(q, kn) + bias, jnp.zeros((rows, LANES), F32),
                                   sfx_ref[...], stack_chunks=True)
            carry_ref[h] = carry
            acc_ref[h] = jnp.dot(w, vn, preferred_element_type=F32)

    for h in range(N_HEADS):
        q = q_ref[:, head_cols(h)]
        k = kp_ref[pl.ds(h, DECODE_KEYS, stride=N_HEADS), :].astype(BF16)
        v = vp_ref[pl.ds(h, DECODE_KEYS, stride=N_HEADS), :].astype(BF16)
        carry, w = _sb_weights(_scores(q, k), carry_ref[h], sfx_ref[...], stack_chunks=True)
        carry_ref[h] = carry
        acc_ref[h] += jnp.dot(w, v, preferred_element_type=F32)

    @pl.when(c == pl.num_programs(1) - 1)
    def _():
        for h in range(N_HEADS):
            o_ref[:, head_cols(h)] = acc_ref[h].astype(BF16)


def _attn_decode(q, k_new, v_new, k_past, v_past, *, batch, rows, past):
    assert rows <= LANES and past % DECODE_KEYS == 0
    n_chunks = past // DECODE_KEYS
    new_spec = pl.BlockSpec((rows, ATTN_WIDTH), lambda b, c: (b, 0))
    past_spec = pl.BlockSpec((DECODE_KEYS * N_HEADS, HEAD_DIM),
                             lambda b, c: (b * n_chunks + n_chunks - 1 - c, 0))
    return pl.pallas_call(
        functools.partial(_attn_decode_kernel, rows=rows),
        grid=(batch, n_chunks),
        in_specs=[new_spec, new_spec, new_spec, past_spec, past_spec],
        out_specs=new_spec,
        out_shape=jax.ShapeDtypeStruct((batch * rows, ATTN_WIDTH), BF16),
        scratch_shapes=[pltpu.VMEM((CHUNK, CHUNK), BF16),
                        pltpu.VMEM((N_HEADS, rows, LANES), F32),
                        pltpu.VMEM((N_HEADS, rows, HEAD_DIM), F32)],
        compiler_params=_params("arbitrary", "arbitrary"),
        name="attn_decode",
    )(q, k_new, v_new, k_past, v_past)


def _mix_out_kernel(oa_ref, ob_ref, gates_ref, x_ref, wa_ref, wb_ref, wo_ref, h_ref, m_ref):
    chunks = [slice(c * MIX_COLS, (c + 1) * MIX_COLS) for c in range(D_MODEL // MIX_COLS)]
    for cols in chunks:
        ya = jnp.dot(oa_ref[...], wa_ref[:, cols], preferred_element_type=F32)
        yb = jnp.dot(ob_ref[...], wb_ref[:, cols], preferred_element_type=F32)
        gate_b_cols = slice(D_MODEL + cols.start, D_MODEL + cols.stop)
        m_ref[:, cols] = (_sigmoid(gates_ref[:, cols]) * ya
                          + _sigmoid(gates_ref[:, gate_b_cols]) * yb).astype(BF16)
    for cols in chunks:
        h_ref[:, cols] = x_ref[:, cols] + jnp.dot(m_ref[...], wo_ref[:, cols],
                                                  preferred_element_type=F32)


def _mix_out(o_a, o_b, gates, x, w_a, w_b, w_out, *, tm):
    n = x.shape[0]

    def rows(width):
        return pl.BlockSpec((tm, width), lambda i: (i, 0))

    def resident(w):
        return pl.BlockSpec(w.shape, lambda i: (0, 0), pipeline_mode=pl.Buffered(1))

    return pl.pallas_call(
        _mix_out_kernel,
        grid=(n // tm,),
        in_specs=[rows(POOL_WIDTH), rows(ATTN_WIDTH), rows(2 * D_MODEL), rows(D_MODEL),
                  resident(w_a), resident(w_b), resident(w_out)],
        out_specs=rows(D_MODEL),
        out_shape=jax.ShapeDtypeStruct((n, D_MODEL), F32),
        scratch_shapes=[pltpu.VMEM((tm, D_MODEL), BF16)],
        compiler_params=_params("arbitrary"),
        name="mix_out",
    )(o_a, o_b, gates, x, w_a, w_b, w_out)


def _ffn_kernel(h_ref, g_ref, wg_ref, wu_ref, wd_ref, gf_ref, y_ref, n_ref):
    j = pl.program_id(1)
    tm = h_ref.shape[0]
    row_chunks = [slice(r, min(r + NORM_ROWS, tm)) for r in range(0, tm, NORM_ROWS)]

    @pl.when(j == 0)
    def _():
        for rows in row_chunks:
            h = h_ref[rows, :]
            n_ref[rows, :] = _rmsnorm(h, g_ref[...]).astype(BF16)
            y_ref[rows, :] = h

    gate = jnp.dot(n_ref[...], wg_ref[...], preferred_element_type=F32)
    up = jnp.dot(n_ref[...], wu_ref[...], preferred_element_type=F32)
    hid = (gate * _sigmoid(gate) * up).astype(BF16)
    y_ref[...] += jnp.dot(hid, wd_ref[...], preferred_element_type=F32)

    @pl.when(j == pl.num_programs(1) - 1)
    def _():
        for rows in row_chunks:
            y_ref[rows, :] = _rmsnorm(y_ref[rows, :], gf_ref[...])


def _ffn(h, g_ffn, w_gate_up, w_down, g_final, *, tm):
    n = h.shape[0]
    tn = WEIGHT_TILE
    nj = D_FF // tn
    return pl.pallas_call(
        _ffn_kernel,
        grid=(n // tm, nj),
        in_specs=[
            pl.BlockSpec((tm, D_MODEL), lambda i, j: (i, 0)),
            pl.BlockSpec((1, D_MODEL), lambda i, j: (0, 0)),
            pl.BlockSpec((D_MODEL, tn), lambda i, j: (0, j)),
            pl.BlockSpec((D_MODEL, tn), lambda i, j: (0, nj + j)),
            pl.BlockSpec((tn, D_MODEL), lambda i, j: (j, 0)),
            pl.BlockSpec((1, D_MODEL), lambda i, j: (0, 0)),
        ],
        out_specs=pl.BlockSpec((tm, D_MODEL), lambda i, j: (i, 0)),
        out_shape=jax.ShapeDtypeStruct((n, D_MODEL), F32),
        scratch_shapes=[pltpu.VMEM((tm, D_MODEL), BF16)],
        compiler_params=_params("arbitrary", "arbitrary", vmem_limit=FFN_VMEM_LIMIT),
        name="ffn",
    )(h, g_ffn, w_gate_up, w_gate_up, w_down, g_final)


def _layer(x, hist, past_kv, weights, g_final, *, batch, seq, pos0):
    g_mix, w_in, w_pool, s_pool, w_a, w_b, w_out, g_ffn, w_gate_up, w_down = weights
    n = batch * seq
    tm = min(n, ROW_BLOCK)
    u, q, k, v, gates = _in_proj(x, g_mix, w_in, tm=tm)
    o_a = _pool(u, hist, w_pool, s_pool, batch=batch, seq=seq, tm=min(seq, POOL_ROWS), pos0=pos0)
    if past_kv is None:
        o_b = _attn_prompt(q, k, v, batch=batch, seq=seq)
    else:
        o_b = _attn_decode(q, k, v, past_kv[0], past_kv[1], batch=batch, rows=seq, past=pos0)
    h = _mix_out(o_a, o_b, gates, x, w_a, w_b, w_out, tm=min(n, MIX_ROWS))
    y = _ffn(h, g_ffn, w_gate_up, w_down, g_final, tm=tm)
    return y, u, k, v


def kernel(x_prompt, x_sample, cache_k, cache_v, state_pool, g_mix, w_in, w_pool, s_pool,
           w_branch, w_out, g_ffn, w_gate_up, w_down, g_final):
    depth = w_in.shape[0]
    assert depth == 1
    b_p, t_p, _ = x_prompt.shape
    b_s, t_s, _ = x_sample.shape
    past = cache_k.shape[2]
    l = 0
    weights = (
        g_mix[l][None, :], w_in[l].astype(BF16), w_pool[l].astype(BF16), s_pool[l][None, :],
        w_branch[l, :POOL_WIDTH].astype(BF16), w_branch[l, POOL_WIDTH:].astype(BF16),
        w_out[l].astype(BF16), g_ffn[l][None, :], w_gate_up[l].astype(BF16),
        w_down[l].astype(BF16),
    )
    g_fin = g_final[None, :]

    hist_p = jnp.zeros((b_p, HIST_ROWS, POOL_WIDTH), F32)
    y_p, u_p, k_p, v_p = _layer(
        x_prompt.reshape(b_p * t_p, D_MODEL), hist_p, None, weights, g_fin,
        batch=b_p, seq=t_p, pos0=0)

    hist_s = jnp.pad(state_pool[l], ((0, 0), (HIST_ROWS - POOL_HIST, 0), (0, 0)))
    past_kv = (cache_k[l].reshape(b_s * past * N_HEADS, HEAD_DIM),
               cache_v[l].reshape(b_s * past * N_HEADS, HEAD_DIM))
    y_s, u_s, k_s, v_s = _layer(
        x_sample.reshape(b_s * t_s, D_MODEL), hist_s, past_kv, weights, g_fin,
        batch=b_s, seq=t_s, pos0=past)

    new_pool_p = u_p.reshape(b_p, t_p, POOL_WIDTH)[:, t_p - POOL_HIST:]
    new_pool_s = jnp.concatenate([state_pool[l], u_s.reshape(b_s, t_s, POOL_WIDTH)],
                                 axis=1)[:, -POOL_HIST:]
    return (
        y_p.reshape(b_p, t_p, D_MODEL),
        y_s.reshape(b_s, t_s, D_MODEL),
        k_p.reshape(1, b_p, t_p, N_HEADS, HEAD_DIM),
        v_p.reshape(1, b_p, t_p, N_HEADS, HEAD_DIM),
        new_pool_p[None],
        k_s.reshape(1, b_s, t_s, N_HEADS, HEAD_DIM),
        v_s.reshape(1, b_s, t_s, N_HEADS, HEAD_DIM),
        new_pool_s[None],
    )
```

```python
import functools

import jax
import jax.numpy as jnp
from jax import lax
from jax.experimental import pallas as pl
from jax.experimental.pallas import tpu as pltpu

F32 = jnp.float32
BF16 = jnp.bfloat16

D_MODEL = 2048
N_HEADS = 8
HEAD_DIM = 128
ATTN_WIDTH = N_HEADS * HEAD_DIM
POOL_WINDOWS = (2, 4, 8, 16)
POOL_WIDTH = D_MODEL // 2
POOL_GROUP_WIDTH = POOL_WIDTH // len(POOL_WINDOWS)
POOL_HIST = max(POOL_WINDOWS) - 1
HIST_ROWS = POOL_HIST + 1
GROUP_WIDTH = 1024
N_GATE_GROUPS = 2 * D_MODEL // GROUP_WIDTH
D_FF = 5632
EPS = 1e-6

LANES = 128
CHUNK = 2 * LANES
DECODE_KEYS = 1024
Q_TILE = 512
Q_SCALE = 1.4426950408889634 * HEAD_DIM ** -0.5
VMEM_LIMIT = 56 * 1024 * 1024
FFN_VMEM_LIMIT = 60 * 1024 * 1024

ROW_BLOCK = 1024
WEIGHT_TILE = 512
MIX_ROWS = 512
MIX_COLS = 512
POOL_ROWS = 512
NORM_ROWS = 256


def _params(*sem, vmem_limit=VMEM_LIMIT):
    return pltpu.CompilerParams(dimension_semantics=sem, vmem_limit_bytes=vmem_limit)


def _rmsnorm(x, g):
    ms = jnp.mean(x * x, axis=-1, keepdims=True)
    return x * lax.rsqrt(ms + EPS) * g


def _sigmoid(x):
    return 1.0 / (1.0 + jnp.exp(-x))


def _inproj_kernel(x_ref, g_ref, w_ref, u_ref, q_ref, k_ref, v_ref, gate_ref, xn_ref, *, tpg):
    j = pl.program_id(1)

    @pl.when(j == 0)
    def _():
        xn_ref[...] = _rmsnorm(x_ref[...], g_ref[...]).astype(BF16)

    def proj():
        return jnp.dot(xn_ref[...], w_ref[...], preferred_element_type=F32)

    @pl.when(j < tpg)
    def _():
        u_ref[...] = proj()

    @pl.when((j >= tpg) & (j < 2 * tpg))
    def _():
        q_ref[...] = (proj() * Q_SCALE).astype(BF16)

    @pl.when((j >= 2 * tpg) & (j < 3 * tpg))
    def _():
        k_ref[...] = proj()

    @pl.when((j >= 3 * tpg) & (j < 4 * tpg))
    def _():
        v_ref[...] = proj()

    @pl.when(j >= 4 * tpg)
    def _():
        gate_ref[...] = proj()


def _in_proj(x, g_mix, w_in, *, tm):
    n = x.shape[0]
    tn = WEIGHT_TILE
    tpg = GROUP_WIDTH // tn
    n_tiles = w_in.shape[1] // tn
    n_blocks = n // tm

    def group_spec(first, ntiles):
        def index(i, j):
            moved_on = (j >= first + ntiles) & (i + 1 < n_blocks)
            return (jnp.where(moved_on, i + 1, i),
                    jnp.where(moved_on, 0, jnp.clip(j - first, 0, ntiles - 1)))

        return pl.BlockSpec((tm, tn), index)

    def group_shape(dtype):
        return jax.ShapeDtypeStruct((n, GROUP_WIDTH), dtype)

    return pl.pallas_call(
        functools.partial(_inproj_kernel, tpg=tpg),
        grid=(n_blocks, n_tiles),
        in_specs=[
            pl.BlockSpec((tm, D_MODEL),
                         lambda i, j: (jnp.where(j > 0, jnp.minimum(i + 1, n_blocks - 1), i), 0)),
            pl.BlockSpec((1, D_MODEL), lambda i, j: (0, 0)),
            pl.BlockSpec((D_MODEL, tn), lambda i, j: (0, j)),
        ],
        out_specs=[group_spec(0, tpg), group_spec(tpg, tpg), group_spec(2 * tpg, tpg),
                   group_spec(3 * tpg, tpg), group_spec(4 * tpg, N_GATE_GROUPS * tpg)],
        out_shape=[group_shape(F32), group_shape(BF16), group_shape(F32), group_shape(F32),
                   jax.ShapeDtypeStruct((n, N_GATE_GROUPS * GROUP_WIDTH), F32)],
        scratch_shapes=[pltpu.VMEM((tm, D_MODEL), BF16)],
        compiler_params=_params("arbitrary", "arbitrary"),
        name="in_proj",
    )(x, g_mix, w_in)


def _pool_kernel(u_ref, uprev_ref, hist_ref, wp_ref, sp_ref, o_ref, buf_ref, *, tm, pos0):
    i = pl.program_id(1)
    buf_ref[HIST_ROWS:, :] = u_ref[...]

    @pl.when(i == 0)
    def _():
        buf_ref[:HIST_ROWS, :] = hist_ref[0]

    @pl.when(i > 0)
    def _():
        buf_ref[:HIST_ROWS, :] = uprev_ref[...]

    pos = pos0 + i * tm + lax.broadcasted_iota(jnp.int32, (tm, 1), 0)
    for g, w in enumerate(POOL_WINDOWS):
        cols = slice(g * POOL_GROUP_WIDTH, (g + 1) * POOL_GROUP_WIDTH)
        x = buf_ref[:, cols]
        s, m = x, 1
        while m < w:
            s = s + pltpu.roll(s, m, axis=0)
            m *= 2
        cur = x[HIST_ROWS:]
        cnt = jnp.minimum(pos + 1, w).astype(F32)
        diff = s[HIST_ROWS:] / cnt - cur
        o = jnp.dot(diff.astype(BF16), wp_ref[g], preferred_element_type=F32)
        o_ref[:, cols] = (o * sp_ref[:, cols]).astype(BF16)


def _pool(u, hist, w_pool, s_pool, *, batch, seq, tm, pos0):
    assert all(w & (w - 1) == 0 for w in POOL_WINDOWS)
    nt = seq // tm
    per = tm // HIST_ROWS
    return pl.pallas_call(
        functools.partial(_pool_kernel, tm=tm, pos0=pos0),
        grid=(batch, nt),
        in_specs=[
            pl.BlockSpec((tm, POOL_WIDTH), lambda b, i: (b * nt + i, 0)),
            pl.BlockSpec((HIST_ROWS, POOL_WIDTH),
                         lambda b, i: (jnp.maximum((b * nt + i) * per - 1, 0), 0)),
            pl.BlockSpec((1, HIST_ROWS, POOL_WIDTH), lambda b, i: (b, 0, 0)),
            pl.BlockSpec(w_pool.shape, lambda b, i: (0, 0, 0)),
            pl.BlockSpec((1, POOL_WIDTH), lambda b, i: (0, 0)),
        ],
        out_specs=pl.BlockSpec((tm, POOL_WIDTH), lambda b, i: (b * nt + i, 0)),
        out_shape=jax.ShapeDtypeStruct((batch * seq, POOL_WIDTH), BF16),
        scratch_shapes=[pltpu.VMEM((HIST_ROWS + tm, POOL_WIDTH), F32)],
        compiler_params=_params("arbitrary", "arbitrary"),
        name="pool",
    )(u, u, hist, w_pool, s_pool)


def _suffix_matrix():
    r = lax.broadcasted_iota(jnp.int32, (CHUNK, CHUNK), 0)
    c = lax.broadcasted_iota(jnp.int32, (CHUNK, CHUNK), 1)
    return jnp.where(r >= c, 1.0, 0.0).astype(BF16)


def _scores(q, kblk):
    return lax.dot_general(q, kblk, (((1,), (1,)), ((), ())), preferred_element_type=F32)


def _sb_weights(z, carry, sfx, *, stack_chunks):
    rows, span = z.shape
    n_chunks = span // CHUNK
    sp = (jnp.maximum(z, 0.0) + jnp.log2(1.0 + jnp.exp2(-jnp.abs(z)))).astype(BF16)

    def chunk(x, c):
        return x[:, c * CHUNK:(c + 1) * CHUNK]

    if stack_chunks:
        r_all = jnp.dot(jnp.concatenate([chunk(sp, c) for c in range(n_chunks)], axis=0), sfx,
                        preferred_element_type=F32)
    ws = []
    for c in reversed(range(n_chunks)):
        if stack_chunks:
            r = r_all[c * rows:(c + 1) * rows]
        else:
            r = jnp.dot(chunk(sp, c), sfx, preferred_element_type=F32)
        w = jnp.exp2(chunk(z, c) - (r + jnp.concatenate([carry] * (CHUNK // LANES), axis=1)))
        ws.insert(0, w.astype(BF16))
        carry = carry + jnp.broadcast_to(r[:, :1], (rows, LANES))
    return carry, jnp.concatenate(ws, axis=1)


def _causal_bias(rows, span):
    r = lax.broadcasted_iota(jnp.int32, (rows, span), 0)
    c = lax.broadcasted_iota(jnp.int32, (rows, span), 1)
    return jnp.where(c < r, 0.0, MASKED_LOGIT)


ATTN_UNROLL = 2
MASKED_LOGIT = -1e9


def _attn_prompt_kernel(q_ref, k_ref, v_ref, o_ref, kb_ref, vt_ref, sfx_ref, bias_ref,
                        z_ref, w_ref, carry_ref, acct_ref, *, seq):
    tile = Q_TILE
    nq = seq // tile
    kb_ref[...] = k_ref[...].astype(BF16)
    for j in range(nq):
        vt_ref[j] = v_ref[j * tile:(j + 1) * tile, :].T.astype(BF16)

    @pl.when((pl.program_id(0) == 0) & (pl.program_id(1) == 0))
    def _():
        sfx_ref[...] = _suffix_matrix()
        bias_ref[...] = _causal_bias(tile, tile)

    def rows(ref, i):
        return ref[pl.ds(pl.multiple_of(i * tile, tile), tile), :]

    def logits(t, slot):
        qi, j = jnp.minimum(t[0], nq - 1), jnp.minimum(t[1], nq - 1)
        z_ref[slot] = _scores(rows(q_ref, qi), rows(kb_ref, j))

    def weights(t, slot, diagonal):
        qi, _ = t
        if not diagonal:
            carry, w = _sb_weights(z_ref[slot], carry_ref[qi], sfx_ref[...], stack_chunks=False)
            carry_ref[qi] = carry
            w_ref[slot] = w
            return
        half = tile // 2
        for r, span in ((slice(0, half), half), (slice(half, tile), tile)):
            carry, w = _sb_weights(z_ref[slot, r, :span] + bias_ref[r, :span],
                                   jnp.zeros((half, LANES), F32), sfx_ref[...], stack_chunks=False)
            carry_ref[qi, r, :] = carry
            w_ref[slot, r, :span] = w
        w_ref[slot, :half, half:] = jnp.zeros((half, half), BF16)

    def value_product(t, slot, store):
        qi, j = t
        pv = lax.dot_general(vt_ref[j], w_ref[slot], (((1,), (1,)), ((), ())),
                             preferred_element_type=F32)
        acct_ref[qi] = pv if store else acct_ref[qi] + pv

    def walk(n_steps, following, diagonal, st):
        assert n_steps % ATTN_UNROLL == 0 and ATTN_UNROLL % 2 == 0

        def group(i, st):
            prev, cur = (st[0], st[1]), (st[2], st[3])
            for s in range(ATTN_UNROLL):
                slot = s % 2
                nxt = following(cur)
                logits(nxt, 1 - slot)
                weights(cur, slot, diagonal)
                value_product(prev, 1 - slot, store=diagonal)
                prev, cur = cur, nxt
            return prev + cur

        return lax.fori_loop(0, n_steps // ATTN_UNROLL, group, st)

    def next_diagonal(t):
        last = t[0] == nq - 1
        return jnp.where(last, 1, t[0] + 1), jnp.where(last, 0, t[1] + 1)

    def next_below(t):
        qi, j = t
        wrap = j == 0
        return jnp.where(wrap, qi + 1, qi), jnp.where(wrap, qi, j - 1)

    zero = jnp.int32(0)
    acct_ref[nq - 1] = jnp.zeros((HEAD_DIM, tile), F32)
    w_ref[1] = jnp.zeros((tile, tile), BF16)
    logits((zero, zero), 0)
    st = walk(nq, next_diagonal, True, (zero, zero, zero, zero))
    st = walk(nq * (nq - 1) // 2, next_below, False, st)
    value_product((st[0], st[1]), 1, store=False)
    for qi in range(nq):
        o_ref[qi * tile:(qi + 1) * tile, :] = acct_ref[qi].T.astype(BF16)


def _attn_prompt(q, k, v, *, batch, seq):
    assert seq % Q_TILE == 0
    spec = pl.BlockSpec((seq, HEAD_DIM), lambda b, h: (b, h))
    return pl.pallas_call(
        functools.partial(_attn_prompt_kernel, seq=seq),
        grid=(batch, N_HEADS),
        in_specs=[spec, spec, spec],
        out_specs=spec,
        out_shape=jax.ShapeDtypeStruct((batch * seq, ATTN_WIDTH), BF16),
        scratch_shapes=[
            pltpu.VMEM((seq, HEAD_DIM), BF16),
            pltpu.VMEM((seq // Q_TILE, HEAD_DIM, Q_TILE), BF16),
            pltpu.VMEM((CHUNK, CHUNK), BF16),
            pltpu.VMEM((Q_TILE, Q_TILE), F32),
            pltpu.VMEM((2, Q_TILE, Q_TILE), F32),
            pltpu.VMEM((2, Q_TILE, Q_TILE), BF16),
            pltpu.VMEM((seq // Q_TILE, Q_TILE, LANES), F32),
            pltpu.VMEM((seq // Q_TILE, HEAD_DIM, Q_TILE), F32),
        ],
        compiler_params=_params("arbitrary", "arbitrary"),
        name="attn_prompt",
    )(q, k, v)


def _attn_decode_kernel(q_ref, kn_ref, vn_ref, kp_ref, vp_ref, o_ref, sfx_ref, carry_ref, acc_ref,
                        *, rows):
    c = pl.program_id(1)

    def head_cols(h):
        return slice(h * HEAD_DIM, (h + 1) * HEAD_DIM)

    @pl.when(c == 0)
    def _():
        sfx_ref[...] = _suffix_matrix()
        bias = _causal_bias(rows, CHUNK)
        pad = jnp.zeros((CHUNK - rows, HEAD_DIM), BF16)
        for h in range(N_HEADS):
            q = q_ref[:, head_cols(h)]
            kn = jnp.concatenate([kn_ref[:, head_cols(h)].astype(BF16), pad], axis=0)
            vn = jnp.concatenate([vn_ref[:, head_cols(h)].astype(BF16), pad], axis=0)
            carry, w = _sb_weights(_scores(q, kn) + bias, jnp.zeros((rows, LANES), F32),
                                   sfx_ref[...], stack_chunks=True)
            carry_ref[h] = carry
            acc_ref[h] = jnp.dot(w, vn, preferred_element_type=F32)

    for h in range(N_HEADS):
        q = q_ref[:, head_cols(h)]
        k = kp_ref[pl.ds(h, DECODE_KEYS, stride=N_HEADS), :].astype(BF16)
        v = vp_ref[pl.ds(h, DECODE_KEYS, stride=N_HEADS), :].astype(BF16)
        carry, w = _sb_weights(_scores(q, k), carry_ref[h], sfx_ref[...], stack_chunks=True)
        carry_ref[h] = carry
        acc_ref[h] += jnp.dot(w, v, preferred_element_type=F32)

    @pl.when(c == pl.num_programs(1) - 1)
    def _():
        for h in range(N_HEADS):
            o_ref[:, head_cols(h)] = acc_ref[h].astype(BF16)


def _attn_decode(q, k_new, v_new, k_past, v_past, *, batch, rows, past):
    assert rows <= LANES and past % DECODE_KEYS == 0
    n_chunks = past // DECODE_KEYS
    new_spec = pl.BlockSpec((rows, ATTN_WIDTH), lambda b, c: (b, 0))
    past_spec = pl.BlockSpec((DECODE_KEYS * N_HEADS, HEAD_DIM),
                             lambda b, c: (b * n_chunks + n_chunks - 1 - c, 0))
    return pl.pallas_call(
        functools.partial(_attn_decode_kernel, rows=rows),
        grid=(batch, n_chunks),
        in_specs=[new_spec, new_spec, new_spec, past_spec, past_spec],
        out_specs=new_spec,
        out_shape=jax.ShapeDtypeStruct((batch * rows, ATTN_WIDTH), BF16),
        scratch_shapes=[pltpu.VMEM((CHUNK, CHUNK), BF16),
                        pltpu.VMEM((N_HEADS, rows, LANES), F32),
                        pltpu.VMEM((N_HEADS, rows, HEAD_DIM), F32)],
        compiler_params=_params("arbitrary", "arbitrary"),
        name="attn_decode",
    )(q, k_new, v_new, k_past, v_past)


def _mix_out_kernel(oa_ref, ob_ref, gates_ref, x_ref, wa_ref, wb_ref, wo_ref, h_ref, m_ref):
    chunks = [slice(c * MIX_COLS, (c + 1) * MIX_COLS) for c in range(D_MODEL // MIX_COLS)]
    for cols in chunks:
        ya = jnp.dot(oa_ref[...], wa_ref[:, cols], preferred_element_type=F32)
        yb = jnp.dot(ob_ref[...], wb_ref[:, cols], preferred_element_type=F32)
        gate_b_cols = slice(D_MODEL + cols.start, D_MODEL + cols.stop)
        m_ref[:, cols] = (_sigmoid(gates_ref[:, cols]) * ya
                          + _sigmoid(gates_ref[:, gate_b_cols]) * yb).astype(BF16)
    for cols in chunks:
        h_ref[:, cols] = x_ref[:, cols] + jnp.dot(m_ref[...], wo_ref[:, cols],
                                                  preferred_element_type=F32)


def _mix_out(o_a, o_b, gates, x, w_a, w_b, w_out, *, tm):
    n = x.shape[0]

    def rows(width):
        return pl.BlockSpec((tm, width), lambda i: (i, 0))

    def resident(w):
        return pl.BlockSpec(w.shape, lambda i: (0, 0), pipeline_mode=pl.Buffered(1))

    return pl.pallas_call(
        _mix_out_kernel,
        grid=(n // tm,),
        in_specs=[rows(POOL_WIDTH), rows(ATTN_WIDTH), rows(2 * D_MODEL), rows(D_MODEL),
                  resident(w_a), resident(w_b), resident(w_out)],
        out_specs=rows(D_MODEL),
        out_shape=jax.ShapeDtypeStruct((n, D_MODEL), F32),
        scratch_shapes=[pltpu.VMEM((tm, D_MODEL), BF16)],
        compiler_params=_params("arbitrary"),
        name="mix_out",
    )(o_a, o_b, gates, x, w_a, w_b, w_out)


def _ffn_kernel(h_ref, g_ref, wg_ref, wu_ref, wd_ref, gf_ref, y_ref, n_ref):
    j = pl.program_id(1)
    tm = h_ref.shape[0]
    row_chunks = [slice(r, min(r + NORM_ROWS, tm)) for r in range(0, tm, NORM_ROWS)]

    @pl.when(j == 0)
    def _():
        for rows in row_chunks:
            h = h_ref[rows, :]
            n_ref[rows, :] = _rmsnorm(h, g_ref[...]).astype(BF16)
            y_ref[rows, :] = h

    gate = jnp.dot(n_ref[...], wg_ref[...], preferred_element_type=F32)
    up = jnp.dot(n_ref[...], wu_ref[...], preferred_element_type=F32)
    hid = (gate * _sigmoid(gate) * up).astype(BF16)
    y_ref[...] += jnp.dot(hid, wd_ref[...], preferred_element_type=F32)

    @pl.when(j == pl.num_programs(1) - 1)
    def _():
        for rows in row_chunks:
            y_ref[rows, :] = _rmsnorm(y_ref[rows, :], gf_ref[...])


def _ffn(h, g_ffn, w_gate_up, w_down, g_final, *, tm):
    n = h.shape[0]
    tn = WEIGHT_TILE
    nj = D_FF // tn
    return pl.pallas_call(
        _ffn_kernel,
        grid=(n // tm, nj),
        in_specs=[
            pl.BlockSpec((tm, D_MODEL), lambda i, j: (i, 0)),
            pl.BlockSpec((1, D_MODEL), lambda i, j: (0, 0)),
            pl.BlockSpec((D_MODEL, tn), lambda i, j: (0, j)),
            pl.BlockSpec((D_MODEL, tn), lambda i, j: (0, nj + j)),
            pl.BlockSpec((tn, D_MODEL), lambda i, j: (j, 0)),
            pl.BlockSpec((1, D_MODEL), lambda i, j: (0, 0)),
        ],
        out_specs=pl.BlockSpec((tm, D_MODEL), lambda i, j: (i, 0)),
        out_shape=jax.ShapeDtypeStruct((n, D_MODEL), F32),
        scratch_shapes=[pltpu.VMEM((tm, D_MODEL), BF16)],
        compiler_params=_params("arbitrary", "arbitrary", vmem_limit=FFN_VMEM_LIMIT),
        name="ffn",
    )(h, g_ffn, w_gate_up, w_gate_up, w_down, g_final)


def _layer(x, hist, past_kv, weights, g_final, *, batch, seq, pos0):
    g_mix, w_in, w_pool, s_pool, w_a, w_b, w_out, g_ffn, w_gate_up, w_down = weights
    n = batch * seq
    tm = min(n, ROW_BLOCK)
    u, q, k, v, gates = _in_proj(x, g_mix, w_in, tm=tm)
    o_a = _pool(u, hist, w_pool, s_pool, batch=batch, seq=seq, tm=min(seq, POOL_ROWS), pos0=pos0)
    if past_kv is None:
        o_b = _attn_prompt(q, k, v, batch=batch, seq=seq)
    else:
        o_b = _attn_decode(q, k, v, past_kv[0], past_kv[1], batch=batch, rows=seq, past=pos0)
    h = _mix_out(o_a, o_b, gates, x, w_a, w_b, w_out, tm=min(n, MIX_ROWS))
    y = _ffn(h, g_ffn, w_gate_up, w_down, g_final, tm=tm)
    return y, u, k, v


def kernel(x_prompt, x_sample, cache_k, cache_v, state_pool, g_mix, w_in, w_pool, s_pool,
           w_branch, w_out, g_ffn, w_gate_up, w_down, g_final):
    depth = w_in.shape[0]
    assert depth == 1
    b_p, t_p, _ = x_prompt.shape
    b_s, t_s, _ = x_sample.shape
    past = cache_k.shape[2]
    l = 0
    weights = (
        g_mix[l][None, :], w_in[l].astype(BF16), w_pool[l].astype(BF16), s_pool[l][None, :],
        w_branch[l, :POOL_WIDTH].astype(BF16), w_branch[l, POOL_WIDTH:].astype(BF16),
        w_out[l].astype(BF16), g_ffn[l][None, :], w_gate_up[l].astype(BF16),
        w_down[l].astype(BF16),
    )
    g_fin = g_final[None, :]

    hist_p = jnp.zeros((b_p, HIST_ROWS, POOL_WIDTH), F32)
    y_p, u_p, k_p, v_p = _layer(
        x_prompt.reshape(b_p * t_p, D_MODEL), hist_p, None, weights, g_fin,
        batch=b_p, seq=t_p, pos0=0)

    hist_s = jnp.pad(state_pool[l], ((0, 0), (HIST_ROWS - POOL_HIST, 0), (0, 0)))
    past_kv = (cache_k[l].reshape(b_s * past * N_HEADS, HEAD_DIM),
               cache_v[l].reshape(b_s * past * N_HEADS, HEAD_DIM))
    y_s, u_s, k_s, v_s = _layer(
        x_sample.reshape(b_s * t_s, D_MODEL), hist_s, past_kv, weights, g_fin,
        batch=b_s, seq=t_s, pos0=past)

    new_pool_p = u_p.reshape(b_p, t_p, POOL_WIDTH)[:, t_p - POOL_HIST:]
    new_pool_s = jnp.concatenate([state_pool[l], u_s.reshape(b_s, t_s, POOL_WIDTH)],
                                 axis=1)[:, -POOL_HIST:]
    return (
        y_p.reshape(b_p, t_p, D_MODEL),
        y_s.reshape(b_s, t_s, D_MODEL),
        k_p.reshape(1, b_p, t_p, N_HEADS, HEAD_DIM),
        v_p.reshape(1, b_p, t_p, N_HEADS, HEAD_DIM),
        new_pool_p[None],
        k_s.reshape(1, b_s, t_s, N_HEADS, HEAD_DIM),
        v_s.reshape(1, b_s, t_s, N_HEADS, HEAD_DIM),
        new_pool_s[None],
    )
```

```python
import functools

import jax
import jax.numpy as jnp
from jax import lax
from jax.experimental import pallas as pl
from jax.experimental.pallas import tpu as pltpu

F32 = jnp.float32
BF16 = jnp.bfloat16

D_MODEL = 2048
N_HEADS = 8
HEAD_DIM = 128
ATTN_WIDTH = N_HEADS * HEAD_DIM
POOL_WINDOWS = (2, 4, 8, 16)
POOL_WIDTH = D_MODEL // 2
POOL_GROUP_WIDTH = POOL_WIDTH // len(POOL_WINDOWS)
POOL_HIST = max(POOL_WINDOWS) - 1
HIST_ROWS = POOL_HIST + 1
GROUP_WIDTH = 1024
N_GATE_GROUPS = 2 * D_MODEL // GROUP_WIDTH
D_FF = 5632
EPS = 1e-6

LANES = 128
CHUNK = 2 * LANES
DECODE_KEYS = 2048
Q_TILE = 512
Q_SCALE = 1.4426950408889634 * HEAD_DIM ** -0.5
VMEM_LIMIT = 56 * 1024 * 1024
FFN_VMEM_LIMIT = 60 * 1024 * 1024

ROW_BLOCK = 1024
WEIGHT_TILE = 512
MIX_ROWS = 512
MIX_COLS = 512
POOL_ROWS = 1024
NORM_ROWS = 256


def _params(*sem, vmem_limit=VMEM_LIMIT):
    return pltpu.CompilerParams(dimension_semantics=sem, vmem_limit_bytes=vmem_limit)


def _rmsnorm(x, g):
    ms = jnp.mean(x * x, axis=-1, keepdims=True)
    return x * lax.rsqrt(ms + EPS) * g


def _sigmoid(x):
    return 1.0 / (1.0 + jnp.exp(-x))


def _inproj_kernel(x_ref, g_ref, w_ref, u_ref, q_ref, k_ref, v_ref, gate_ref, xn_ref, *, tpg):
    j = pl.program_id(1)

    @pl.when(j == 0)
    def _():
        xn_ref[...] = _rmsnorm(x_ref[...], g_ref[...]).astype(BF16)

    def proj():
        return jnp.dot(xn_ref[...], w_ref[...], preferred_element_type=F32)

    @pl.when(j < tpg)
    def _():
        u_ref[...] = proj()

    @pl.when((j >= tpg) & (j < 2 * tpg))
    def _():
        q_ref[...] = (proj() * Q_SCALE).astype(BF16)

    @pl.when((j >= 2 * tpg) & (j < 3 * tpg))
    def _():
        k_ref[...] = proj()

    @pl.when((j >= 3 * tpg) & (j < 4 * tpg))
    def _():
        v_ref[...] = proj()

    @pl.when(j >= 4 * tpg)
    def _():
        gate_ref[...] = proj()


def _in_proj(x, g_mix, w_in, *, tm):
    n = x.shape[0]
    tn = WEIGHT_TILE
    tpg = GROUP_WIDTH // tn
    n_tiles = (4 + N_GATE_GROUPS) * tpg
    n_blocks = n // tm

    def group_spec(first, ntiles):
        def index(i, j):
            moved_on = (j >= first + ntiles) & (i + 1 < n_blocks)
            return (jnp.where(moved_on, i + 1, i),
                    jnp.where(moved_on, 0, jnp.clip(j - first, 0, ntiles - 1)))

        return pl.BlockSpec((tm, tn), index)

    def group_shape(dtype):
        return jax.ShapeDtypeStruct((n, GROUP_WIDTH), dtype)

    return pl.pallas_call(
        functools.partial(_inproj_kernel, tpg=tpg),
        grid=(n_blocks, n_tiles),
        in_specs=[
            pl.BlockSpec((tm, D_MODEL),
                         lambda i, j: (jnp.where(j > 0, jnp.minimum(i + 1, n_blocks - 1), i), 0)),
            pl.BlockSpec((1, D_MODEL), lambda i, j: (0, 0)),
            pl.BlockSpec((D_MODEL, tn), lambda i, j: (0, j)),
        ],
        out_specs=[group_spec(0, tpg), group_spec(tpg, tpg), group_spec(2 * tpg, tpg),
                   group_spec(3 * tpg, tpg), group_spec(4 * tpg, N_GATE_GROUPS * tpg)],
        out_shape=[group_shape(F32), group_shape(BF16), group_shape(F32), group_shape(F32),
                   jax.ShapeDtypeStruct((n, N_GATE_GROUPS * GROUP_WIDTH), F32)],
        scratch_shapes=[pltpu.VMEM((tm, D_MODEL), BF16)],
        compiler_params=_params("arbitrary", "arbitrary"),
        name="in_proj",
    )(x, g_mix, w_in)


def _pool_kernel(u_ref, uprev_ref, hist_ref, wp_ref, sp_ref, o_ref, buf_ref, *, tm, pos0):
    i = pl.program_id(1)
    buf_ref[HIST_ROWS:, :] = u_ref[...]

    @pl.when(i == 0)
    def _():
        buf_ref[:HIST_ROWS, :] = hist_ref[0]

    @pl.when(i > 0)
    def _():
        buf_ref[:HIST_ROWS, :] = uprev_ref[...]

    pos = pos0 + i * tm + lax.broadcasted_iota(jnp.int32, (tm, 1), 0)
    for g, w in enumerate(POOL_WINDOWS):
        cols = slice(g * POOL_GROUP_WIDTH, (g + 1) * POOL_GROUP_WIDTH)
        x = buf_ref[:, cols]
        s, m = x, 1
        while m < w:
            s = s + pltpu.roll(s, m, axis=0)
            m *= 2
        cur = x[HIST_ROWS:]
        cnt = jnp.minimum(pos + 1, w).astype(F32)
        diff = s[HIST_ROWS:] / cnt - cur
        o = jnp.dot(diff.astype(BF16), wp_ref[g], preferred_element_type=F32)
        o_ref[:, cols] = (o * sp_ref[:, cols]).astype(BF16)


def _pool(u, hist, w_pool, s_pool, *, batch, seq, tm, pos0):
    assert all(w & (w - 1) == 0 for w in POOL_WINDOWS)
    nt = seq // tm
    per = tm // HIST_ROWS
    return pl.pallas_call(
        functools.partial(_pool_kernel, tm=tm, pos0=pos0),
        grid=(batch, nt),
        in_specs=[
            pl.BlockSpec((tm, POOL_WIDTH), lambda b, i: (b * nt + i, 0)),
            pl.BlockSpec((HIST_ROWS, POOL_WIDTH),
                         lambda b, i: (jnp.maximum((b * nt + i) * per - 1, 0), 0)),
            pl.BlockSpec((1, HIST_ROWS, POOL_WIDTH), lambda b, i: (b, 0, 0)),
            pl.BlockSpec(w_pool.shape, lambda b, i: (0, 0, 0)),
            pl.BlockSpec((1, POOL_WIDTH), lambda b, i: (0, 0)),
        ],
        out_specs=pl.BlockSpec((tm, POOL_WIDTH), lambda b, i: (b * nt + i, 0)),
        out_shape=jax.ShapeDtypeStruct((batch * seq, POOL_WIDTH), BF16),
        scratch_shapes=[pltpu.VMEM((HIST_ROWS + tm, POOL_WIDTH), F32)],
        compiler_params=_params("arbitrary", "arbitrary"),
        name="pool",
    )(u, u, hist, w_pool, s_pool)


def _suffix_matrix():
    r = lax.broadcasted_iota(jnp.int32, (CHUNK, CHUNK), 0)
    c = lax.broadcasted_iota(jnp.int32, (CHUNK, CHUNK), 1)
    return jnp.where(r >= c, 1.0, 0.0).astype(BF16)


def _scores(q, kblk):
    return lax.dot_general(q, kblk, (((1,), (1,)), ((), ())), preferred_element_type=F32)


def _sb_weights(z, carry, sfx, *, stack_chunks):
    rows, span = z.shape
    n_chunks = span // CHUNK
    sp = (jnp.maximum(z, 0.0) + jnp.log2(1.0 + jnp.exp2(-jnp.abs(z)))).astype(BF16)

    def chunk(x, c):
        return x[:, c * CHUNK:(c + 1) * CHUNK]

    if stack_chunks:
        r_all = jnp.dot(jnp.concatenate([chunk(sp, c) for c in range(n_chunks)], axis=0), sfx,
                        preferred_element_type=F32)
    ws = []
    for c in reversed(range(n_chunks)):
        if stack_chunks:
            r = r_all[c * rows:(c + 1) * rows]
        else:
            r = jnp.dot(chunk(sp, c), sfx, preferred_element_type=F32)
        w = jnp.exp2(chunk(z, c) - (r + jnp.concatenate([carry] * (CHUNK // LANES), axis=1)))
        ws.insert(0, w.astype(BF16))
        carry = carry + jnp.broadcast_to(r[:, :1], (rows, LANES))
    return carry, jnp.concatenate(ws, axis=1)


def _causal_bias(rows, span):
    r = lax.broadcasted_iota(jnp.int32, (rows, span), 0)
    c = lax.broadcasted_iota(jnp.int32, (rows, span), 1)
    return jnp.where(c < r, 0.0, MASKED_LOGIT)


ATTN_UNROLL = 2
MASKED_LOGIT = -1e9


def _attn_prompt_kernel(q_ref, k_ref, v_ref, o_ref, kb_ref, vt_ref, sfx_ref, bias_ref,
                        z_ref, w_ref, carry_ref, acct_ref, *, seq):
    tile = Q_TILE
    nq = seq // tile
    kb_ref[...] = k_ref[...].astype(BF16)
    for j in range(nq):
        vt_ref[j] = v_ref[j * tile:(j + 1) * tile, :].T.astype(BF16)

    @pl.when((pl.program_id(0) == 0) & (pl.program_id(1) == 0))
    def _():
        sfx_ref[...] = _suffix_matrix()
        bias_ref[...] = _causal_bias(tile, tile)

    def rows(ref, i):
        return ref[pl.ds(pl.multiple_of(i * tile, tile), tile), :]

    def logits(t, slot):
        qi, j = jnp.minimum(t[0], nq - 1), jnp.minimum(t[1], nq - 1)
        z_ref[slot] = _scores(rows(q_ref, qi), rows(kb_ref, j))

    def weights(t, slot, diagonal):
        qi, _ = t
        if not diagonal:
            carry, w = _sb_weights(z_ref[slot], carry_ref[qi], sfx_ref[...], stack_chunks=False)
            carry_ref[qi] = carry
            w_ref[slot] = w
            return
        half = tile // 2
        for r, span in ((slice(0, half), half), (slice(half, tile), tile)):
            carry, w = _sb_weights(z_ref[slot, r, :span] + bias_ref[r, :span],
                                   jnp.zeros((half, LANES), F32), sfx_ref[...], stack_chunks=False)
            carry_ref[qi, r, :] = carry
            w_ref[slot, r, :span] = w
        w_ref[slot, :half, half:] = jnp.zeros((half, half), BF16)

    def value_product(t, slot, store):
        qi, j = t
        pv = lax.dot_general(vt_ref[j], w_ref[slot], (((1,), (1,)), ((), ())),
                             preferred_element_type=F32)
        acct_ref[qi] = pv if store else acct_ref[qi] + pv

    def walk(n_steps, following, diagonal, st):
        assert n_steps % ATTN_UNROLL == 0 and ATTN_UNROLL % 2 == 0

        def group(i, st):
            prev, cur = (st[0], st[1]), (st[2], st[3])
            for s in range(ATTN_UNROLL):
                slot = s % 2
                nxt = following(cur)
                logits(nxt, 1 - slot)
                weights(cur, slot, diagonal)
                value_product(prev, 1 - slot, store=diagonal)
                prev, cur = cur, nxt
            return prev + cur

        return lax.fori_loop(0, n_steps // ATTN_UNROLL, group, st)

    def next_diagonal(t):
        last = t[0] == nq - 1
        return jnp.where(last, 1, t[0] + 1), jnp.where(last, 0, t[1] + 1)

    def next_below(t):
        qi, j = t
        wrap = j == 0
        return jnp.where(wrap, qi + 1, qi), jnp.where(wrap, qi, j - 1)

    zero = jnp.int32(0)
    acct_ref[nq - 1] = jnp.zeros((HEAD_DIM, tile), F32)
    w_ref[1] = jnp.zeros((tile, tile), BF16)
    logits((zero, zero), 0)
    st = walk(nq, next_diagonal, True, (zero, zero, zero, zero))
    st = walk(nq * (nq - 1) // 2, next_below, False, st)
    value_product((st[0], st[1]), 1, store=False)
    for qi in range(nq):
        o_ref[qi * tile:(qi + 1) * tile, :] = acct_ref[qi].T.astype(BF16)


def _attn_prompt(q, k, v, *, batch, seq):
    assert seq % Q_TILE == 0
    spec = pl.BlockSpec((seq, HEAD_DIM), lambda b, h: (b, h))
    return pl.pallas_call(
        functools.partial(_attn_prompt_kernel, seq=seq),
        grid=(batch, N_HEADS),
        in_specs=[spec, spec, spec],
        out_specs=spec,
        out_shape=jax.ShapeDtypeStruct((batch * seq, ATTN_WIDTH), BF16),
        scratch_shapes=[
            pltpu.VMEM((seq, HEAD_DIM), BF16),
            pltpu.VMEM((seq // Q_TILE, HEAD_DIM, Q_TILE), BF16),
            pltpu.VMEM((CHUNK, CHUNK), BF16),
            pltpu.VMEM((Q_TILE, Q_TILE), F32),
            pltpu.VMEM((2, Q_TILE, Q_TILE), F32),
            pltpu.VMEM((2, Q_TILE, Q_TILE), BF16),
            pltpu.VMEM((seq // Q_TILE, Q_TILE, LANES), F32),
            pltpu.VMEM((seq // Q_TILE, HEAD_DIM, Q_TILE), F32),
        ],
        compiler_params=_params("arbitrary", "arbitrary"),
        name="attn_prompt",
    )(q, k, v)


def _attn_decode_kernel(q_ref, kn_ref, vn_ref, kp_ref, vp_ref, o_ref, sfx_ref, carry_ref, acc_ref,
                        *, rows):
    c = pl.program_id(1)

    def head_cols(h):
        return slice(h * HEAD_DIM, (h + 1) * HEAD_DIM)

    @pl.when(c == 0)
    def _():
        sfx_ref[...] = _suffix_matrix()
        bias = _causal_bias(rows, CHUNK)
        pad = jnp.zeros((CHUNK - rows, HEAD_DIM), BF16)
        for h in range(N_HEADS):
            q = q_ref[:, head_cols(h)]
            kn = jnp.concatenate([kn_ref[:, head_cols(h)].astype(BF16), pad], axis=0)
            vn = jnp.concatenate([vn_ref[:, head_cols(h)].astype(BF16), pad], axis=0)
            carry, w = _sb_weights(_scores(q, kn) + bias, jnp.zeros((rows, LANES), F32),
                                   sfx_ref[...], stack_chunks=True)
            carry_ref[h] = carry
            acc_ref[h] = jnp.dot(w, vn, preferred_element_type=F32)

    for h in range(N_HEADS):
        q = q_ref[:, head_cols(h)]
        k = kp_ref[pl.ds(h, DECODE_KEYS, stride=N_HEADS), :].astype(BF16)
        v = vp_ref[pl.ds(h, DECODE_KEYS, stride=N_HEADS), :].astype(BF16)
        carry, w = _sb_weights(_scores(q, k), carry_ref[h], sfx_ref[...], stack_chunks=True)
        carry_ref[h] = carry
        acc_ref[h] += jnp.dot(w, v, preferred_element_type=F32)

    @pl.when(c == pl.num_programs(1) - 1)
    def _():
        for h in range(N_HEADS):
            o_ref[:, head_cols(h)] = acc_ref[h].astype(BF16)


def _attn_decode(q, k_new, v_new, k_past, v_past, *, batch, rows, past):
    assert rows <= LANES and past % DECODE_KEYS == 0
    n_chunks = past // DECODE_KEYS
    new_spec = pl.BlockSpec((rows, ATTN_WIDTH), lambda b, c: (b, 0))
    past_spec = pl.BlockSpec((DECODE_KEYS * N_HEADS, HEAD_DIM),
                             lambda b, c: (b * n_chunks + n_chunks - 1 - c, 0))
    return pl.pallas_call(
        functools.partial(_attn_decode_kernel, rows=rows),
        grid=(batch, n_chunks),
        in_specs=[new_spec, new_spec, new_spec, past_spec, past_spec],
        out_specs=new_spec,
        out_shape=jax.ShapeDtypeStruct((batch * rows, ATTN_WIDTH), BF16),
        scratch_shapes=[pltpu.VMEM((CHUNK, CHUNK), BF16),
                        pltpu.VMEM((N_HEADS, rows, LANES), F32),
                        pltpu.VMEM((N_HEADS, rows, HEAD_DIM), F32)],
        compiler_params=_params("arbitrary", "arbitrary"),
        name="attn_decode",
    )(q, k_new, v_new, k_past, v_past)


def _mix_out_kernel(oa_ref, ob_ref, gates_ref, x_ref, wa_ref, wb_ref, wo_ref, h_ref, m_ref):
    chunks = [slice(c * MIX_COLS, (c + 1) * MIX_COLS) for c in range(D_MODEL // MIX_COLS)]
    for cols in chunks:
        ya = jnp.dot(oa_ref[...], wa_ref[:, cols], preferred_element_type=F32)
        yb = jnp.dot(ob_ref[...], wb_ref[:, cols], preferred_element_type=F32)
        gate_b_cols = slice(D_MODEL + cols.start, D_MODEL + cols.stop)
        m_ref[:, cols] = (_sigmoid(gates_ref[:, cols]) * ya
                          + _sigmoid(gates_ref[:, gate_b_cols]) * yb).astype(BF16)
    for cols in chunks:
        h_ref[:, cols] = x_ref[:, cols] + jnp.dot(m_ref[...], wo_ref[:, cols],
                                                  preferred_element_type=F32)


def _mix_out(o_a, o_b, gates, x, w_a, w_b, w_out, *, tm):
    n = x.shape[0]

    def rows(width):
        return pl.BlockSpec((tm, width), lambda i: (i, 0))

    def resident(w):
        return pl.BlockSpec(w.shape, lambda i: (0, 0), pipeline_mode=pl.Buffered(1))

    return pl.pallas_call(
        _mix_out_kernel,
        grid=(n // tm,),
        in_specs=[rows(POOL_WIDTH), rows(ATTN_WIDTH), rows(2 * D_MODEL), rows(D_MODEL),
                  resident(w_a), resident(w_b), resident(w_out)],
        out_specs=rows(D_MODEL),
        out_shape=jax.ShapeDtypeStruct((n, D_MODEL), F32),
        scratch_shapes=[pltpu.VMEM((tm, D_MODEL), BF16)],
        compiler_params=_params("arbitrary"),
        name="mix_out",
    )(o_a, o_b, gates, x, w_a, w_b, w_out)


def _ffn_kernel(h_ref, g_ref, wg_ref, wu_ref, wd_ref, gf_ref, y_ref, n_ref):
    j = pl.program_id(1)
    tm = h_ref.shape[0]
    row_chunks = [slice(r, min(r + NORM_ROWS, tm)) for r in range(0, tm, NORM_ROWS)]

    @pl.when(j == 0)
    def _():
        for rows in row_chunks:
            h = h_ref[rows, :]
            n_ref[rows, :] = _rmsnorm(h, g_ref[...]).astype(BF16)
            y_ref[rows, :] = h

    gate = jnp.dot(n_ref[...], wg_ref[...], preferred_element_type=F32)
    up = jnp.dot(n_ref[...], wu_ref[...], preferred_element_type=F32)
    hid = (gate * _sigmoid(gate) * up).astype(BF16)
    y_ref[...] += jnp.dot(hid, wd_ref[...], preferred_element_type=F32)

    @pl.when(j == pl.num_programs(1) - 1)
    def _():
        for rows in row_chunks:
            y_ref[rows, :] = _rmsnorm(y_ref[rows, :], gf_ref[...])


def _ffn(h, g_ffn, w_gate_up, w_down, g_final, *, tm):
    n = h.shape[0]
    tn = WEIGHT_TILE
    nj = D_FF // tn
    return pl.pallas_call(
        _ffn_kernel,
        grid=(n // tm, nj),
        in_specs=[
            pl.BlockSpec((tm, D_MODEL), lambda i, j: (i, 0)),
            pl.BlockSpec((1, D_MODEL), lambda i, j: (0, 0)),
            pl.BlockSpec((D_MODEL, tn), lambda i, j: (0, j)),
            pl.BlockSpec((D_MODEL, tn), lambda i, j: (0, nj + j)),
            pl.BlockSpec((tn, D_MODEL), lambda i, j: (j, 0)),
            pl.BlockSpec((1, D_MODEL), lambda i, j: (0, 0)),
        ],
        out_specs=pl.BlockSpec((tm, D_MODEL), lambda i, j: (i, 0)),
        out_shape=jax.ShapeDtypeStruct((n, D_MODEL), F32),
        scratch_shapes=[pltpu.VMEM((tm, D_MODEL), BF16)],
        compiler_params=_params("arbitrary", "arbitrary", vmem_limit=FFN_VMEM_LIMIT),
        name="ffn",
    )(h, g_ffn, w_gate_up, w_gate_up, w_down, g_final)


def _odd_pitch(w):
    return jnp.pad(w, ((0, 0), (0, WEIGHT_TILE)))


def _layer(x, hist, past_kv, weights, g_final, *, batch, seq, pos0):
    g_mix, w_in, w_pool, s_pool, w_a, w_b, w_out, g_ffn, w_gate_up, w_down = weights
    n = batch * seq
    tm = min(n, ROW_BLOCK)
    u, q, k, v, gates = _in_proj(x, g_mix, w_in, tm=tm)
    o_a = _pool(u, hist, w_pool, s_pool, batch=batch, seq=seq, tm=min(seq, POOL_ROWS), pos0=pos0)
    if past_kv is None:
        o_b = _attn_prompt(q, k, v, batch=batch, seq=seq)
    else:
        o_b = _attn_decode(q, k, v, past_kv[0], past_kv[1], batch=batch, rows=seq, past=pos0)
    h = _mix_out(o_a, o_b, gates, x, w_a, w_b, w_out, tm=min(n, MIX_ROWS))
    y = _ffn(h, g_ffn, w_gate_up, w_down, g_final, tm=tm)
    return y, u, k, v


def kernel(x_prompt, x_sample, cache_k, cache_v, state_pool, g_mix, w_in, w_pool, s_pool,
           w_branch, w_out, g_ffn, w_gate_up, w_down, g_final):
    depth = w_in.shape[0]
    assert depth == 1
    b_p, t_p, _ = x_prompt.shape
    b_s, t_s, _ = x_sample.shape
    past = cache_k.shape[2]
    l = 0
    weights = (
        g_mix[l][None, :], _odd_pitch(w_in[l].astype(BF16)), w_pool[l].astype(BF16),
        s_pool[l][None, :],
        w_branch[l, :POOL_WIDTH].astype(BF16), w_branch[l, POOL_WIDTH:].astype(BF16),
        w_out[l].astype(BF16), g_ffn[l][None, :], w_gate_up[l].astype(BF16),
        w_down[l].astype(BF16),
    )
    g_fin = g_final[None, :]

    hist_p = jnp.zeros((b_p, HIST_ROWS, POOL_WIDTH), F32)
    y_p, u_p, k_p, v_p = _layer(
        x_prompt.reshape(b_p * t_p, D_MODEL), hist_p, None, weights, g_fin,
        batch=b_p, seq=t_p, pos0=0)

    hist_s = jnp.pad(state_pool[l], ((0, 0), (HIST_ROWS - POOL_HIST, 0), (0, 0)))
    past_kv = (cache_k[l].reshape(b_s * past * N_HEADS, HEAD_DIM),
               cache_v[l].reshape(b_s * past * N_HEADS, HEAD_DIM))
    y_s, u_s, k_s, v_s = _layer(
        x_sample.reshape(b_s * t_s, D_MODEL), hist_s, past_kv, weights, g_fin,
        batch=b_s, seq=t_s, pos0=past)

    new_pool_p = u_p.reshape(b_p, t_p, POOL_WIDTH)[:, t_p - POOL_HIST:]
    new_pool_s = jnp.concatenate([state_pool[l], u_s.reshape(b_s, t_s, POOL_WIDTH)],
                                 axis=1)[:, -POOL_HIST:]
    return (
        y_p.reshape(b_p, t_p, D_MODEL),
        y_s.reshape(b_s, t_s, D_MODEL),
        k_p.reshape(1, b_p, t_p, N_HEADS, HEAD_DIM),
        v_p.reshape(1, b_p, t_p, N_HEADS, HEAD_DIM),
        new_pool_p[None],
        k_s.reshape(1, b_s, t_s, N_HEADS, HEAD_DIM),
        v_s.reshape(1, b_s, t_s, N_HEADS, HEAD_DIM),
        new_pool_s[None],
    )
```

```python
import functools

import jax
import jax.numpy as jnp
from jax import lax
from jax.experimental import pallas as pl
from jax.experimental.pallas import tpu as pltpu

F32 = jnp.float32
BF16 = jnp.bfloat16

D_MODEL = 2048
N_HEADS = 8
HEAD_DIM = 128
ATTN_WIDTH = N_HEADS * HEAD_DIM
POOL_WINDOWS = (2, 4, 8, 16)
POOL_WIDTH = D_MODEL // 2
POOL_GROUP_WIDTH = POOL_WIDTH // len(POOL_WINDOWS)
POOL_HIST = max(POOL_WINDOWS) - 1
HIST_ROWS = POOL_HIST + 1
GROUP_WIDTH = 1024
N_GATE_GROUPS = 2 * D_MODEL // GROUP_WIDTH
D_FF = 5632
EPS = 1e-6

LANES = 128
CHUNK = 2 * LANES
DECODE_KEYS = 2048
Q_TILE = 512
Q_SCALE = 1.4426950408889634 * HEAD_DIM ** -0.5
VMEM_LIMIT = 56 * 1024 * 1024
FFN_VMEM_LIMIT = 60 * 1024 * 1024

ROW_BLOCK = 1024
WEIGHT_TILE = 512
MIX_ROWS = 512
MIX_COLS = 512
POOL_ROWS = 1024
NORM_ROWS = 256


def _params(*sem, vmem_limit=VMEM_LIMIT):
    return pltpu.CompilerParams(dimension_semantics=sem, vmem_limit_bytes=vmem_limit)


def _rmsnorm(x, g):
    ms = jnp.mean(x * x, axis=-1, keepdims=True)
    return x * lax.rsqrt(ms + EPS) * g


def _sigmoid(x):
    return 1.0 / (1.0 + jnp.exp(-x))


def _inproj_kernel(x_ref, g_ref, w_ref, u_ref, q_ref, k_ref, v_ref, gate_ref, xn_ref, *, tpg):
    j = pl.program_id(1)

    @pl.when(j == 0)
    def _():
        xn_ref[...] = _rmsnorm(x_ref[...], g_ref[...]).astype(BF16)

    def proj():
        return lax.dot_general(xn_ref[...], w_ref[...], (((1,), (1,)), ((), ())),
                               preferred_element_type=F32)

    @pl.when(j < tpg)
    def _():
        u_ref[...] = proj()

    @pl.when((j >= tpg) & (j < 2 * tpg))
    def _():
        q_ref[...] = (proj() * Q_SCALE).astype(BF16)

    @pl.when((j >= 2 * tpg) & (j < 3 * tpg))
    def _():
        k_ref[...] = proj()

    @pl.when((j >= 3 * tpg) & (j < 4 * tpg))
    def _():
        v_ref[...] = proj()

    @pl.when(j >= 4 * tpg)
    def _():
        gate_ref[...] = proj()


def _in_proj(x, g_mix, w_in_t, *, tm):
    n = x.shape[0]
    tn = WEIGHT_TILE
    tpg = GROUP_WIDTH // tn
    n_tiles = w_in_t.shape[0] // tn
    n_blocks = n // tm

    def group_spec(first, ntiles):
        def index(i, j):
            moved_on = (j >= first + ntiles) & (i + 1 < n_blocks)
            return (jnp.where(moved_on, i + 1, i),
                    jnp.where(moved_on, 0, jnp.clip(j - first, 0, ntiles - 1)))

        return pl.BlockSpec((tm, tn), index)

    def group_shape(dtype):
        return jax.ShapeDtypeStruct((n, GROUP_WIDTH), dtype)

    return pl.pallas_call(
        functools.partial(_inproj_kernel, tpg=tpg),
        grid=(n_blocks, n_tiles),
        in_specs=[
            pl.BlockSpec((tm, D_MODEL),
                         lambda i, j: (jnp.where(j > 0, jnp.minimum(i + 1, n_blocks - 1), i), 0)),
            pl.BlockSpec((1, D_MODEL), lambda i, j: (0, 0)),
            pl.BlockSpec((tn, D_MODEL), lambda i, j: (j, 0)),
        ],
        out_specs=[group_spec(0, tpg), group_spec(tpg, tpg), group_spec(2 * tpg, tpg),
                   group_spec(3 * tpg, tpg), group_spec(4 * tpg, N_GATE_GROUPS * tpg)],
        out_shape=[group_shape(F32), group_shape(BF16), group_shape(F32), group_shape(F32),
                   jax.ShapeDtypeStruct((n, N_GATE_GROUPS * GROUP_WIDTH), F32)],
        scratch_shapes=[pltpu.VMEM((tm, D_MODEL), BF16)],
        compiler_params=_params("arbitrary", "arbitrary"),
        name="in_proj",
    )(x, g_mix, w_in_t)


def _pool_kernel(u_ref, uprev_ref, hist_ref, wp_ref, sp_ref, o_ref, buf_ref, *, tm, pos0):
    i = pl.program_id(1)
    buf_ref[HIST_ROWS:, :] = u_ref[...]

    @pl.when(i == 0)
    def _():
        buf_ref[:HIST_ROWS, :] = hist_ref[0]

    @pl.when(i > 0)
    def _():
        buf_ref[:HIST_ROWS, :] = uprev_ref[...]

    pos = pos0 + i * tm + lax.broadcasted_iota(jnp.int32, (tm, 1), 0)
    for g, w in enumerate(POOL_WINDOWS):
        cols = slice(g * POOL_GROUP_WIDTH, (g + 1) * POOL_GROUP_WIDTH)
        x = buf_ref[:, cols]
        s, m = x, 1
        while m < w:
            s = s + pltpu.roll(s, m, axis=0)
            m *= 2
        cur = x[HIST_ROWS:]
        cnt = jnp.minimum(pos + 1, w).astype(F32)
        diff = s[HIST_ROWS:] / cnt - cur
        o = jnp.dot(diff.astype(BF16), wp_ref[g], preferred_element_type=F32)
        o_ref[:, cols] = (o * sp_ref[:, cols]).astype(BF16)


def _pool(u, hist, w_pool, s_pool, *, batch, seq, tm, pos0):
    assert all(w & (w - 1) == 0 for w in POOL_WINDOWS)
    nt = seq // tm
    per = tm // HIST_ROWS
    return pl.pallas_call(
        functools.partial(_pool_kernel, tm=tm, pos0=pos0),
        grid=(batch, nt),
        in_specs=[
            pl.BlockSpec((tm, POOL_WIDTH), lambda b, i: (b * nt + i, 0)),
            pl.BlockSpec((HIST_ROWS, POOL_WIDTH),
                         lambda b, i: (jnp.maximum((b * nt + i) * per - 1, 0), 0)),
            pl.BlockSpec((1, HIST_ROWS, POOL_WIDTH), lambda b, i: (b, 0, 0)),
            pl.BlockSpec(w_pool.shape, lambda b, i: (0, 0, 0)),
            pl.BlockSpec((1, POOL_WIDTH), lambda b, i: (0, 0)),
        ],
        out_specs=pl.BlockSpec((tm, POOL_WIDTH), lambda b, i: (b * nt + i, 0)),
        out_shape=jax.ShapeDtypeStruct((batch * seq, POOL_WIDTH), BF16),
        scratch_shapes=[pltpu.VMEM((HIST_ROWS + tm, POOL_WIDTH), F32)],
        compiler_params=_params("arbitrary", "arbitrary"),
        name="pool",
    )(u, u, hist, w_pool, s_pool)


def _suffix_matrix():
    r = lax.broadcasted_iota(jnp.int32, (CHUNK, CHUNK), 0)
    c = lax.broadcasted_iota(jnp.int32, (CHUNK, CHUNK), 1)
    return jnp.where(r >= c, 1.0, 0.0).astype(BF16)


def _scores(q, kblk):
    return lax.dot_general(q, kblk, (((1,), (1,)), ((), ())), preferred_element_type=F32)


def _sb_weights(z, carry, sfx, *, stack_chunks):
    rows, span = z.shape
    n_chunks = span // CHUNK
    sp = (jnp.maximum(z, 0.0) + jnp.log2(1.0 + jnp.exp2(-jnp.abs(z)))).astype(BF16)

    def chunk(x, c):
        return x[:, c * CHUNK:(c + 1) * CHUNK]

    if stack_chunks:
        r_all = jnp.dot(jnp.concatenate([chunk(sp, c) for c in range(n_chunks)], axis=0), sfx,
                        preferred_element_type=F32)
    ws = []
    for c in reversed(range(n_chunks)):
        if stack_chunks:
            r = r_all[c * rows:(c + 1) * rows]
        else:
            r = jnp.dot(chunk(sp, c), sfx, preferred_element_type=F32)
        w = jnp.exp2(chunk(z, c) - (r + jnp.concatenate([carry] * (CHUNK // LANES), axis=1)))
        ws.insert(0, w.astype(BF16))
        carry = carry + jnp.broadcast_to(r[:, :1], (rows, LANES))
    return carry, jnp.concatenate(ws, axis=1)


def _causal_bias(rows, span):
    r = lax.broadcasted_iota(jnp.int32, (rows, span), 0)
    c = lax.broadcasted_iota(jnp.int32, (rows, span), 1)
    return jnp.where(c < r, 0.0, MASKED_LOGIT)


ATTN_UNROLL = 2
MASKED_LOGIT = -1e9


def _attn_prompt_kernel(q_ref, k_ref, v_ref, o_ref, kb_ref, vt_ref, sfx_ref, bias_ref,
                        z_ref, w_ref, carry_ref, acct_ref, *, seq):
    tile = Q_TILE
    nq = seq // tile
    kb_ref[...] = k_ref[...].astype(BF16)
    for j in range(nq):
        vt_ref[j] = v_ref[j * tile:(j + 1) * tile, :].T.astype(BF16)

    @pl.when((pl.program_id(0) == 0) & (pl.program_id(1) == 0))
    def _():
        sfx_ref[...] = _suffix_matrix()
        bias_ref[...] = _causal_bias(tile, tile)

    def rows(ref, i):
        return ref[pl.ds(pl.multiple_of(i * tile, tile), tile), :]

    def logits(t, slot):
        qi, j = jnp.minimum(t[0], nq - 1), jnp.minimum(t[1], nq - 1)
        z_ref[slot] = _scores(rows(q_ref, qi), rows(kb_ref, j))

    def weights(t, slot, diagonal):
        qi, _ = t
        if not diagonal:
            carry, w = _sb_weights(z_ref[slot], carry_ref[qi], sfx_ref[...], stack_chunks=False)
            carry_ref[qi] = carry
            w_ref[slot] = w
            return
        half = tile // 2
        for r, span in ((slice(0, half), half), (slice(half, tile), tile)):
            carry, w = _sb_weights(z_ref[slot, r, :span] + bias_ref[r, :span],
                                   jnp.zeros((half, LANES), F32), sfx_ref[...], stack_chunks=False)
            carry_ref[qi, r, :] = carry
            w_ref[slot, r, :span] = w
        w_ref[slot, :half, half:] = jnp.zeros((half, half), BF16)

    def value_product(t, slot, store):
        qi, j = t
        pv = lax.dot_general(vt_ref[j], w_ref[slot], (((1,), (1,)), ((), ())),
                             preferred_element_type=F32)
        acct_ref[qi] = pv if store else acct_ref[qi] + pv

    def walk(n_steps, following, diagonal, st):
        assert n_steps % ATTN_UNROLL == 0 and ATTN_UNROLL % 2 == 0

        def group(i, st):
            prev, cur = (st[0], st[1]), (st[2], st[3])
            for s in range(ATTN_UNROLL):
                slot = s % 2
                nxt = following(cur)
                logits(nxt, 1 - slot)
                weights(cur, slot, diagonal)
                value_product(prev, 1 - slot, store=diagonal)
                prev, cur = cur, nxt
            return prev + cur

        return lax.fori_loop(0, n_steps // ATTN_UNROLL, group, st)

    def next_diagonal(t):
        last = t[0] == nq - 1
        return jnp.where(last, 1, t[0] + 1), jnp.where(last, 0, t[1] + 1)

    def next_below(t):
        qi, j = t
        wrap = j == 0
        return jnp.where(wrap, qi + 1, qi), jnp.where(wrap, qi, j - 1)

    zero = jnp.int32(0)
    acct_ref[nq - 1] = jnp.zeros((HEAD_DIM, tile), F32)
    w_ref[1] = jnp.zeros((tile, tile), BF16)
    logits((zero, zero), 0)
    st = walk(nq, next_diagonal, True, (zero, zero, zero, zero))
    st = walk(nq * (nq - 1) // 2, next_below, False, st)
    value_product((st[0], st[1]), 1, store=False)
    for qi in range(nq):
        o_ref[qi * tile:(qi + 1) * tile, :] = acct_ref[qi].T.astype(BF16)


def _attn_prompt(q, k, v, *, batch, seq):
    assert seq % Q_TILE == 0
    spec = pl.BlockSpec((seq, HEAD_DIM), lambda b, h: (b, h))
    return pl.pallas_call(
        functools.partial(_attn_prompt_kernel, seq=seq),
        grid=(batch, N_HEADS),
        in_specs=[spec, spec, spec],
        out_specs=spec,
        out_shape=jax.ShapeDtypeStruct((batch * seq, ATTN_WIDTH), BF16),
        scratch_shapes=[
            pltpu.VMEM((seq, HEAD_DIM), BF16),
            pltpu.VMEM((seq // Q_TILE, HEAD_DIM, Q_TILE), BF16),
            pltpu.VMEM((CHUNK, CHUNK), BF16),
            pltpu.VMEM((Q_TILE, Q_TILE), F32),
            pltpu.VMEM((2, Q_TILE, Q_TILE), F32),
            pltpu.VMEM((2, Q_TILE, Q_TILE), BF16),
            pltpu.VMEM((seq // Q_TILE, Q_TILE, LANES), F32),
            pltpu.VMEM((seq // Q_TILE, HEAD_DIM, Q_TILE), F32),
        ],
        compiler_params=_params("arbitrary", "arbitrary"),
        name="attn_prompt",
    )(q, k, v)


def _attn_decode_kernel(q_ref, kn_ref, vn_ref, kp_ref, vp_ref, o_ref, sfx_ref, carry_ref, acc_ref,
                        *, rows):
    c = pl.program_id(1)

    def head_cols(h):
        return slice(h * HEAD_DIM, (h + 1) * HEAD_DIM)

    @pl.when(c == 0)
    def _():
        sfx_ref[...] = _suffix_matrix()
        bias = _causal_bias(rows, CHUNK)
        pad = jnp.zeros((CHUNK - rows, HEAD_DIM), BF16)
        for h in range(N_HEADS):
            q = q_ref[:, head_cols(h)]
            kn = jnp.concatenate([kn_ref[:, head_cols(h)].astype(BF16), pad], axis=0)
            vn = jnp.concatenate([vn_ref[:, head_cols(h)].astype(BF16), pad], axis=0)
            carry, w = _sb_weights(_scores(q, kn) + bias, jnp.zeros((rows, LANES), F32),
                                   sfx_ref[...], stack_chunks=True)
            carry_ref[h] = carry
            acc_ref[h] = jnp.dot(w, vn, preferred_element_type=F32)

    for h in range(N_HEADS):
        q = q_ref[:, head_cols(h)]
        k = kp_ref[pl.ds(h, DECODE_KEYS, stride=N_HEADS), :].astype(BF16)
        v = vp_ref[pl.ds(h, DECODE_KEYS, stride=N_HEADS), :].astype(BF16)
        carry, w = _sb_weights(_scores(q, k), carry_ref[h], sfx_ref[...], stack_chunks=True)
        carry_ref[h] = carry
        acc_ref[h] += jnp.dot(w, v, preferred_element_type=F32)

    @pl.when(c == pl.num_programs(1) - 1)
    def _():
        for h in range(N_HEADS):
            o_ref[:, head_cols(h)] = acc_ref[h].astype(BF16)


def _attn_decode(q, k_new, v_new, k_past, v_past, *, batch, rows, past):
    assert rows <= LANES and past % DECODE_KEYS == 0
    n_chunks = past // DECODE_KEYS
    new_spec = pl.BlockSpec((rows, ATTN_WIDTH), lambda b, c: (b, 0))
    past_spec = pl.BlockSpec((DECODE_KEYS * N_HEADS, HEAD_DIM),
                             lambda b, c: (b * n_chunks + n_chunks - 1 - c, 0))
    return pl.pallas_call(
        functools.partial(_attn_decode_kernel, rows=rows),
        grid=(batch, n_chunks),
        in_specs=[new_spec, new_spec, new_spec, past_spec, past_spec],
        out_specs=new_spec,
        out_shape=jax.ShapeDtypeStruct((batch * rows, ATTN_WIDTH), BF16),
        scratch_shapes=[pltpu.VMEM((CHUNK, CHUNK), BF16),
                        pltpu.VMEM((N_HEADS, rows, LANES), F32),
                        pltpu.VMEM((N_HEADS, rows, HEAD_DIM), F32)],
        compiler_params=_params("arbitrary", "arbitrary"),
        name="attn_decode",
    )(q, k_new, v_new, k_past, v_past)


def _mix_out_kernel(oa_ref, ob_ref, gates_ref, x_ref, wa_ref, wb_ref, wo_ref, h_ref, m_ref):
    chunks = [slice(c * MIX_COLS, (c + 1) * MIX_COLS) for c in range(D_MODEL // MIX_COLS)]
    for cols in chunks:
        ya = jnp.dot(oa_ref[...], wa_ref[:, cols], preferred_element_type=F32)
        yb = jnp.dot(ob_ref[...], wb_ref[:, cols], preferred_element_type=F32)
        gate_b_cols = slice(D_MODEL + cols.start, D_MODEL + cols.stop)
        m_ref[:, cols] = (_sigmoid(gates_ref[:, cols]) * ya
                          + _sigmoid(gates_ref[:, gate_b_cols]) * yb).astype(BF16)
    for cols in chunks:
        h_ref[:, cols] = x_ref[:, cols] + jnp.dot(m_ref[...], wo_ref[:, cols],
                                                  preferred_element_type=F32)


def _mix_out(o_a, o_b, gates, x, w_a, w_b, w_out, *, tm):
    n = x.shape[0]

    def rows(width):
        return pl.BlockSpec((tm, width), lambda i: (i, 0))

    def resident(w):
        return pl.BlockSpec(w.shape, lambda i: (0, 0), pipeline_mode=pl.Buffered(1))

    return pl.pallas_call(
        _mix_out_kernel,
        grid=(n // tm,),
        in_specs=[rows(POOL_WIDTH), rows(ATTN_WIDTH), rows(2 * D_MODEL), rows(D_MODEL),
                  resident(w_a), resident(w_b), resident(w_out)],
        out_specs=rows(D_MODEL),
        out_shape=jax.ShapeDtypeStruct((n, D_MODEL), F32),
        scratch_shapes=[pltpu.VMEM((tm, D_MODEL), BF16)],
        compiler_params=_params("arbitrary"),
        name="mix_out",
    )(o_a, o_b, gates, x, w_a, w_b, w_out)


def _ffn_kernel(h_ref, g_ref, wg_ref, wu_ref, wd_ref, gf_ref, y_ref, n_ref):
    j = pl.program_id(1)
    tm = h_ref.shape[0]
    row_chunks = [slice(r, min(r + NORM_ROWS, tm)) for r in range(0, tm, NORM_ROWS)]

    @pl.when(j == 0)
    def _():
        for rows in row_chunks:
            h = h_ref[rows, :]
            n_ref[rows, :] = _rmsnorm(h, g_ref[...]).astype(BF16)
            y_ref[rows, :] = h

    gate = jnp.dot(n_ref[...], wg_ref[...], preferred_element_type=F32)
    up = jnp.dot(n_ref[...], wu_ref[...], preferred_element_type=F32)
    hid = (gate * _sigmoid(gate) * up).astype(BF16)
    y_ref[...] += jnp.dot(hid, wd_ref[...], preferred_element_type=F32)

    @pl.when(j == pl.num_programs(1) - 1)
    def _():
        for rows in row_chunks:
            y_ref[rows, :] = _rmsnorm(y_ref[rows, :], gf_ref[...])


def _ffn(h, g_ffn, w_gate_up, w_down, g_final, *, tm):
    n = h.shape[0]
    tn = WEIGHT_TILE
    nj = D_FF // tn
    return pl.pallas_call(
        _ffn_kernel,
        grid=(n // tm, nj),
        in_specs=[
            pl.BlockSpec((tm, D_MODEL), lambda i, j: (i, 0)),
            pl.BlockSpec((1, D_MODEL), lambda i, j: (0, 0)),
            pl.BlockSpec((D_MODEL, tn), lambda i, j: (0, j)),
            pl.BlockSpec((D_MODEL, tn), lambda i, j: (0, nj + j)),
            pl.BlockSpec((tn, D_MODEL), lambda i, j: (j, 0)),
            pl.BlockSpec((1, D_MODEL), lambda i, j: (0, 0)),
        ],
        out_specs=pl.BlockSpec((tm, D_MODEL), lambda i, j: (i, 0)),
        out_shape=jax.ShapeDtypeStruct((n, D_MODEL), F32),
        scratch_shapes=[pltpu.VMEM((tm, D_MODEL), BF16)],
        compiler_params=_params("arbitrary", "arbitrary", vmem_limit=FFN_VMEM_LIMIT),
        name="ffn",
    )(h, g_ffn, w_gate_up, w_gate_up, w_down, g_final)


def _layer(x, hist, past_kv, weights, g_final, *, batch, seq, pos0):
    g_mix, w_in, w_pool, s_pool, w_a, w_b, w_out, g_ffn, w_gate_up, w_down = weights
    n = batch * seq
    tm = min(n, ROW_BLOCK)
    u, q, k, v, gates = _in_proj(x, g_mix, w_in, tm=tm)
    o_a = _pool(u, hist, w_pool, s_pool, batch=batch, seq=seq, tm=min(seq, POOL_ROWS), pos0=pos0)
    if past_kv is None:
        o_b = _attn_prompt(q, k, v, batch=batch, seq=seq)
    else:
        o_b = _attn_decode(q, k, v, past_kv[0], past_kv[1], batch=batch, rows=seq, past=pos0)
    h = _mix_out(o_a, o_b, gates, x, w_a, w_b, w_out, tm=min(n, MIX_ROWS))
    y = _ffn(h, g_ffn, w_gate_up, w_down, g_final, tm=tm)
    return y, u, k, v


def kernel(x_prompt, x_sample, cache_k, cache_v, state_pool, g_mix, w_in, w_pool, s_pool,
           w_branch, w_out, g_ffn, w_gate_up, w_down, g_final):
    depth = w_in.shape[0]
    assert depth == 1
    b_p, t_p, _ = x_prompt.shape
    b_s, t_s, _ = x_sample.shape
    past = cache_k.shape[2]
    l = 0
    weights = (
        g_mix[l][None, :], w_in[l].T.astype(BF16), w_pool[l].astype(BF16), s_pool[l][None, :],
        w_branch[l, :POOL_WIDTH].astype(BF16), w_branch[l, POOL_WIDTH:].astype(BF16),
        w_out[l].astype(BF16), g_ffn[l][None, :], w_gate_up[l].astype(BF16),
        w_down[l].astype(BF16),
    )
    g_fin = g_final[None, :]

    hist_p = jnp.zeros((b_p, HIST_ROWS, POOL_WIDTH), F32)
    y_p, u_p, k_p, v_p = _layer(
        x_prompt.reshape(b_p * t_p, D_MODEL), hist_p, None, weights, g_fin,
        batch=b_p, seq=t_p, pos0=0)

    hist_s = jnp.pad(state_pool[l], ((0, 0), (HIST_ROWS - POOL_HIST, 0), (0, 0)))
    past_kv = (cache_k[l].reshape(b_s * past * N_HEADS, HEAD_DIM),
               cache_v[l].reshape(b_s * past * N_HEADS, HEAD_DIM))
    y_s, u_s, k_s, v_s = _layer(
        x_sample.reshape(b_s * t_s, D_MODEL), hist_s, past_kv, weights, g_fin,
        batch=b_s, seq=t_s, pos0=past)

    new_pool_p = u_p.reshape(b_p, t_p, POOL_WIDTH)[:, t_p - POOL_HIST:]
    new_pool_s = jnp.concatenate([state_pool[l], u_s.reshape(b_s, t_s, POOL_WIDTH)],
                                 axis=1)[:, -POOL_HIST:]
    return (
        y_p.reshape(b_p, t_p, D_MODEL),
        y_s.reshape(b_s, t_s, D_MODEL),
        k_p.reshape(1, b_p, t_p, N_HEADS, HEAD_DIM),
        v_p.reshape(1, b_p, t_p, N_HEADS, HEAD_DIM),
        new_pool_p[None],
        k_s.reshape(1, b_s, t_s, N_HEADS, HEAD_DIM),
        v_s.reshape(1, b_s, t_s, N_HEADS, HEAD_DIM),
        new_pool_s[None],
    )
```

```python
import functools

import jax
import jax.numpy as jnp
from jax import lax
from jax.experimental import pallas as pl
from jax.experimental.pallas import tpu as pltpu

F32 = jnp.float32
BF16 = jnp.bfloat16

D_MODEL = 2048
N_HEADS = 8
HEAD_DIM = 128
ATTN_WIDTH = N_HEADS * HEAD_DIM
POOL_WINDOWS = (2, 4, 8, 16)
POOL_WIDTH = D_MODEL // 2
POOL_GROUP_WIDTH = POOL_WIDTH // len(POOL_WINDOWS)
POOL_HIST = max(POOL_WINDOWS) - 1
HIST_ROWS = POOL_HIST + 1
GROUP_WIDTH = 1024
N_GATE_GROUPS = 2 * D_MODEL // GROUP_WIDTH
D_FF = 5632
EPS = 1e-6

LANES = 128
CHUNK = 2 * LANES
DECODE_KEYS = 2048
Q_TILE = 512
Q_SCALE = 1.4426950408889634 * HEAD_DIM ** -0.5
VMEM_LIMIT = 56 * 1024 * 1024
FFN_VMEM_LIMIT = 60 * 1024 * 1024

ROW_BLOCK = 1024
WEIGHT_TILE = 512
MIX_ROWS = 512
MIX_COLS = 512
POOL_ROWS = 1024
NORM_ROWS = 256


def _params(*sem, vmem_limit=VMEM_LIMIT):
    return pltpu.CompilerParams(dimension_semantics=sem, vmem_limit_bytes=vmem_limit)


def _rmsnorm(x, g):
    ms = jnp.mean(x * x, axis=-1, keepdims=True)
    return x * lax.rsqrt(ms + EPS) * g


def _sigmoid(x):
    return 1.0 / (1.0 + jnp.exp(-x))


def _inproj_kernel(x_ref, g_ref, w_ref, u_ref, q_ref, k_ref, v_ref, gate_ref, xn_ref, *, tpg):
    j = pl.program_id(1)

    @pl.when(j == 0)
    def _():
        xn_ref[...] = _rmsnorm(x_ref[...], g_ref[...]).astype(BF16)

    def proj():
        return jnp.dot(xn_ref[...], w_ref[...], preferred_element_type=F32)

    @pl.when(j < tpg)
    def _():
        u_ref[...] = proj()

    @pl.when((j >= tpg) & (j < 2 * tpg))
    def _():
        q_ref[...] = (proj() * Q_SCALE).astype(BF16)

    @pl.when((j >= 2 * tpg) & (j < 3 * tpg))
    def _():
        k_ref[...] = proj()

    @pl.when((j >= 3 * tpg) & (j < 4 * tpg))
    def _():
        v_ref[...] = proj()

    @pl.when(j >= 4 * tpg)
    def _():
        gate_ref[...] = proj()


def _in_proj(x, g_mix, w_in, *, tm):
    n = x.shape[0]
    tn = WEIGHT_TILE
    tpg = GROUP_WIDTH // tn
    n_tiles = w_in.shape[1] // tn
    n_blocks = n // tm

    def group_spec(first, ntiles):
        def index(i, j):
            moved_on = (j >= first + ntiles) & (i + 1 < n_blocks)
            return (jnp.where(moved_on, i + 1, i),
                    jnp.where(moved_on, 0, jnp.clip(j - first, 0, ntiles - 1)))

        return pl.BlockSpec((tm, tn), index)

    def group_shape(dtype):
        return jax.ShapeDtypeStruct((n, GROUP_WIDTH), dtype)

    return pl.pallas_call(
        functools.partial(_inproj_kernel, tpg=tpg),
        grid=(n_blocks, n_tiles),
        in_specs=[
            pl.BlockSpec((tm, D_MODEL),
                         lambda i, j: (jnp.where(j > 0, jnp.minimum(i + 1, n_blocks - 1), i), 0)),
            pl.BlockSpec((1, D_MODEL), lambda i, j: (0, 0)),
            pl.BlockSpec((D_MODEL, tn), lambda i, j: (0, j)),
        ],
        out_specs=[group_spec(0, tpg), group_spec(tpg, tpg), group_spec(2 * tpg, tpg),
                   group_spec(3 * tpg, tpg), group_spec(4 * tpg, N_GATE_GROUPS * tpg)],
        out_shape=[group_shape(F32), group_shape(BF16), group_shape(F32), group_shape(F32),
                   jax.ShapeDtypeStruct((n, N_GATE_GROUPS * GROUP_WIDTH), F32)],
        scratch_shapes=[pltpu.VMEM((tm, D_MODEL), BF16)],
        compiler_params=_params("arbitrary", "arbitrary"),
        name="in_proj",
    )(x, g_mix, w_in)


def _pool_kernel(u_ref, uprev_ref, hist_ref, wp_ref, sp_ref, o_ref, buf_ref, *, tm, pos0):
    i = pl.program_id(1)
    buf_ref[HIST_ROWS:, :] = u_ref[...]

    @pl.when(i == 0)
    def _():
        buf_ref[:HIST_ROWS, :] = hist_ref[0]

    @pl.when(i > 0)
    def _():
        buf_ref[:HIST_ROWS, :] = uprev_ref[...]

    pos = pos0 + i * tm + lax.broadcasted_iota(jnp.int32, (tm, 1), 0)
    for g, w in enumerate(POOL_WINDOWS):
        cols = slice(g * POOL_GROUP_WIDTH, (g + 1) * POOL_GROUP_WIDTH)
        x = buf_ref[:, cols]
        s, m = x, 1
        while m < w:
            s = s + pltpu.roll(s, m, axis=0)
            m *= 2
        cur = x[HIST_ROWS:]
        cnt = jnp.minimum(pos + 1, w).astype(F32)
        diff = s[HIST_ROWS:] / cnt - cur
        o = jnp.dot(diff.astype(BF16), wp_ref[g], preferred_element_type=F32)
        o_ref[:, cols] = (o * sp_ref[:, cols]).astype(BF16)


def _pool(u, hist, w_pool, s_pool, *, batch, seq, tm, pos0):
    assert all(w & (w - 1) == 0 for w in POOL_WINDOWS)
    nt = seq // tm
    per = tm // HIST_ROWS
    return pl.pallas_call(
        functools.partial(_pool_kernel, tm=tm, pos0=pos0),
        grid=(batch, nt),
        in_specs=[
            pl.BlockSpec((tm, POOL_WIDTH), lambda b, i: (b * nt + i, 0)),
            pl.BlockSpec((HIST_ROWS, POOL_WIDTH),
                         lambda b, i: (jnp.maximum((b * nt + i) * per - 1, 0), 0)),
            pl.BlockSpec((1, HIST_ROWS, POOL_WIDTH), lambda b, i: (b, 0, 0)),
            pl.BlockSpec(w_pool.shape, lambda b, i: (0, 0, 0)),
            pl.BlockSpec((1, POOL_WIDTH), lambda b, i: (0, 0)),
        ],
        out_specs=pl.BlockSpec((tm, POOL_WIDTH), lambda b, i: (b * nt + i, 0)),
        out_shape=jax.ShapeDtypeStruct((batch * seq, POOL_WIDTH), BF16),
        scratch_shapes=[pltpu.VMEM((HIST_ROWS + tm, POOL_WIDTH), F32)],
        compiler_params=_params("arbitrary", "arbitrary"),
        name="pool",
    )(u, u, hist, w_pool, s_pool)


def _suffix_matrix():
    r = lax.broadcasted_iota(jnp.int32, (CHUNK, CHUNK), 0)
    c = lax.broadcasted_iota(jnp.int32, (CHUNK, CHUNK), 1)
    return jnp.where(r >= c, 1.0, 0.0).astype(BF16)


def _scores(q, kblk):
    return lax.dot_general(q, kblk, (((1,), (1,)), ((), ())), preferred_element_type=F32)


def _sb_weights(z, carry, sfx, *, stack_chunks):
    rows, span = z.shape
    n_chunks = span // CHUNK
    sp = (jnp.maximum(z, 0.0) + jnp.log2(1.0 + jnp.exp2(-jnp.abs(z)))).astype(BF16)

    def chunk(x, c):
        return x[:, c * CHUNK:(c + 1) * CHUNK]

    if stack_chunks:
        r_all = jnp.dot(jnp.concatenate([chunk(sp, c) for c in range(n_chunks)], axis=0), sfx,
                        preferred_element_type=F32)
    ws = []
    for c in reversed(range(n_chunks)):
        if stack_chunks:
            r = r_all[c * rows:(c + 1) * rows]
        else:
            r = jnp.dot(chunk(sp, c), sfx, preferred_element_type=F32)
        w = jnp.exp2(chunk(z, c) - (r + jnp.concatenate([carry] * (CHUNK // LANES), axis=1)))
        ws.insert(0, w.astype(BF16))
        carry = carry + jnp.broadcast_to(r[:, :1], (rows, LANES))
    return carry, jnp.concatenate(ws, axis=1)


def _causal_bias(rows, span):
    r = lax.broadcasted_iota(jnp.int32, (rows, span), 0)
    c = lax.broadcasted_iota(jnp.int32, (rows, span), 1)
    return jnp.where(c < r, 0.0, MASKED_LOGIT)


ATTN_UNROLL = 2
MASKED_LOGIT = -1e9


def _attn_prompt_kernel(q_ref, k_ref, v_ref, o_ref, kb_ref, vt_ref, sfx_ref, bias_ref,
                        z_ref, w_ref, carry_ref, acct_ref, *, seq):
    tile = Q_TILE
    nq = seq // tile
    kb_ref[...] = k_ref[...].astype(BF16)
    for j in range(nq):
        vt_ref[j] = v_ref[j * tile:(j + 1) * tile, :].T.astype(BF16)

    @pl.when((pl.program_id(0) == 0) & (pl.program_id(1) == 0))
    def _():
        sfx_ref[...] = _suffix_matrix()
        bias_ref[...] = _causal_bias(tile, tile)

    def rows(ref, i):
        return ref[pl.ds(pl.multiple_of(i * tile, tile), tile), :]

    def logits(t, slot):
        qi, j = jnp.minimum(t[0], nq - 1), jnp.minimum(t[1], nq - 1)
        z_ref[slot] = _scores(rows(q_ref, qi), rows(kb_ref, j))

    def weights(t, slot, diagonal):
        qi, _ = t
        if not diagonal:
            carry, w = _sb_weights(z_ref[slot], carry_ref[qi], sfx_ref[...], stack_chunks=False)
            carry_ref[qi] = carry
            w_ref[slot] = w
            return
        half = tile // 2
        for r, span in ((slice(0, half), half), (slice(half, tile), tile)):
            carry, w = _sb_weights(z_ref[slot, r, :span] + bias_ref[r, :span],
                                   jnp.zeros((half, LANES), F32), sfx_ref[...], stack_chunks=False)
            carry_ref[qi, r, :] = carry
            w_ref[slot, r, :span] = w
        w_ref[slot, :half, half:] = jnp.zeros((half, half), BF16)

    def value_product(t, slot, store):
        qi, j = t
        pv = lax.dot_general(vt_ref[j], w_ref[slot], (((1,), (1,)), ((), ())),
                             preferred_element_type=F32)
        acct_ref[qi] = pv if store else acct_ref[qi] + pv

    def walk(n_steps, following, diagonal, st):
        assert n_steps % ATTN_UNROLL == 0 and ATTN_UNROLL % 2 == 0

        def group(i, st):
            prev, cur = (st[0], st[1]), (st[2], st[3])
            for s in range(ATTN_UNROLL):
                slot = s % 2
                nxt = following(cur)
                logits(nxt, 1 - slot)
                weights(cur, slot, diagonal)
                value_product(prev, 1 - slot, store=diagonal)
                prev, cur = cur, nxt
            return prev + cur

        return lax.fori_loop(0, n_steps // ATTN_UNROLL, group, st)

    def next_diagonal(t):
        last = t[0] == nq - 1
        return jnp.where(last, 1, t[0] + 1), jnp.where(last, 0, t[1] + 1)

    def next_below(t):
        qi, j = t
        wrap = j == 0
        return jnp.where(wrap, qi + 1, qi), jnp.where(wrap, qi, j - 1)

    zero = jnp.int32(0)
    acct_ref[nq - 1] = jnp.zeros((HEAD_DIM, tile), F32)
    w_ref[1] = jnp.zeros((tile, tile), BF16)
    logits((zero, zero), 0)
    st = walk(nq, next_diagonal, True, (zero, zero, zero, zero))
    st = walk(nq * (nq - 1) // 2, next_below, False, st)
    value_product((st[0], st[1]), 1, store=False)
    for qi in range(nq):
        o_ref[qi * tile:(qi + 1) * tile, :] = acct_ref[qi].T.astype(BF16)


def _attn_prompt(q, k, v, *, batch, seq):
    assert seq % Q_TILE == 0
    spec = pl.BlockSpec((seq, HEAD_DIM), lambda b, h: (b, h))
    return pl.pallas_call(
        functools.partial(_attn_prompt_kernel, seq=seq),
        grid=(batch, N_HEADS),
        in_specs=[spec, spec, spec],
        out_specs=spec,
        out_shape=jax.ShapeDtypeStruct((batch * seq, ATTN_WIDTH), BF16),
        scratch_shapes=[
            pltpu.VMEM((seq, HEAD_DIM), BF16),
            pltpu.VMEM((seq // Q_TILE, HEAD_DIM, Q_TILE), BF16),
            pltpu.VMEM((CHUNK, CHUNK), BF16),
            pltpu.VMEM((Q_TILE, Q_TILE), F32),
            pltpu.VMEM((2, Q_TILE, Q_TILE), F32),
            pltpu.VMEM((2, Q_TILE, Q_TILE), BF16),
            pltpu.VMEM((seq // Q_TILE, Q_TILE, LANES), F32),
            pltpu.VMEM((seq // Q_TILE, HEAD_DIM, Q_TILE), F32),
        ],
        compiler_params=_params("arbitrary", "arbitrary"),
        name="attn_prompt",
    )(q, k, v)


def _attn_decode_kernel(q_ref, kn_ref, vn_ref, kp_ref, vp_ref, o_ref, sfx_ref, carry_ref, acc_ref,
                        *, rows):
    c = pl.program_id(1)

    def head_cols(h):
        return slice(h * HEAD_DIM, (h + 1) * HEAD_DIM)

    @pl.when(c == 0)
    def _():
        sfx_ref[...] = _suffix_matrix()
        bias = _causal_bias(rows, CHUNK)
        pad = jnp.zeros((CHUNK - rows, HEAD_DIM), BF16)
        for h in range(N_HEADS):
            q = q_ref[:, head_cols(h)]
            kn = jnp.concatenate([kn_ref[:, head_cols(h)].astype(BF16), pad], axis=0)
            vn = jnp.concatenate([vn_ref[:, head_cols(h)].astype(BF16), pad], axis=0)
            carry, w = _sb_weights(_scores(q, kn) + bias, jnp.zeros((rows, LANES), F32),
                                   sfx_ref[...], stack_chunks=True)
            carry_ref[h] = carry
            acc_ref[h] = jnp.dot(w, vn, preferred_element_type=F32)

    for h in range(N_HEADS):
        q = q_ref[:, head_cols(h)]
        k = kp_ref[pl.ds(h, DECODE_KEYS, stride=N_HEADS), :].astype(BF16)
        v = vp_ref[pl.ds(h, DECODE_KEYS, stride=N_HEADS), :].astype(BF16)
        carry, w = _sb_weights(_scores(q, k), carry_ref[h], sfx_ref[...], stack_chunks=True)
        carry_ref[h] = carry
        acc_ref[h] += jnp.dot(w, v, preferred_element_type=F32)

    @pl.when(c == pl.num_programs(1) - 1)
    def _():
        for h in range(N_HEADS):
            o_ref[:, head_cols(h)] = acc_ref[h].astype(BF16)


def _attn_decode(q, k_new, v_new, k_past, v_past, *, batch, rows, past):
    assert rows <= LANES and past % DECODE_KEYS == 0
    n_chunks = past // DECODE_KEYS
    new_spec = pl.BlockSpec((rows, ATTN_WIDTH), lambda b, c: (b, 0))
    past_spec = pl.BlockSpec((DECODE_KEYS * N_HEADS, HEAD_DIM),
                             lambda b, c: (b * n_chunks + n_chunks - 1 - c, 0))
    return pl.pallas_call(
        functools.partial(_attn_decode_kernel, rows=rows),
        grid=(batch, n_chunks),
        in_specs=[new_spec, new_spec, new_spec, past_spec, past_spec],
        out_specs=new_spec,
        out_shape=jax.ShapeDtypeStruct((batch * rows, ATTN_WIDTH), BF16),
        scratch_shapes=[pltpu.VMEM((CHUNK, CHUNK), BF16),
                        pltpu.VMEM((N_HEADS, rows, LANES), F32),
                        pltpu.VMEM((N_HEADS, rows, HEAD_DIM), F32)],
        compiler_params=_params("arbitrary", "arbitrary"),
        name="attn_decode",
    )(q, k_new, v_new, k_past, v_past)


def _mix_out_kernel(oa_ref, ob_ref, gates_ref, x_ref, wa_ref, wb_ref, wo_ref, h_ref, m_ref):
    chunks = [slice(c * MIX_COLS, (c + 1) * MIX_COLS) for c in range(D_MODEL // MIX_COLS)]
    for cols in chunks:
        ya = jnp.dot(oa_ref[...], wa_ref[:, cols], preferred_element_type=F32)
        yb = jnp.dot(ob_ref[...], wb_ref[:, cols], preferred_element_type=F32)
        gate_b_cols = slice(D_MODEL + cols.start, D_MODEL + cols.stop)
        m_ref[:, cols] = (_sigmoid(gates_ref[:, cols]) * ya
                          + _sigmoid(gates_ref[:, gate_b_cols]) * yb).astype(BF16)
    for cols in chunks:
        h_ref[:, cols] = x_ref[:, cols] + jnp.dot(m_ref[...], wo_ref[:, cols],
                                                  preferred_element_type=F32)


def _mix_out(o_a, o_b, gates, x, w_a, w_b, w_out, *, tm):
    n = x.shape[0]

    def rows(width):
        return pl.BlockSpec((tm, width), lambda i: (i, 0))

    def resident(w):
        return pl.BlockSpec(w.shape, lambda i: (0, 0), pipeline_mode=pl.Buffered(1))

    return pl.pallas_call(
        _mix_out_kernel,
        grid=(n // tm,),
        in_specs=[rows(POOL_WIDTH), rows(ATTN_WIDTH), rows(2 * D_MODEL), rows(D_MODEL),
                  resident(w_a), resident(w_b), resident(w_out)],
        out_specs=rows(D_MODEL),
        out_shape=jax.ShapeDtypeStruct((n, D_MODEL), F32),
        scratch_shapes=[pltpu.VMEM((tm, D_MODEL), BF16)],
        compiler_params=_params("arbitrary"),
        name="mix_out",
    )(o_a, o_b, gates, x, w_a, w_b, w_out)


def _ffn_kernel(h_ref, g_ref, wg_ref, wu_ref, wd_ref, gf_ref, y_ref, n_ref):
    j = pl.program_id(1)
    tm = h_ref.shape[0]
    row_chunks = [slice(r, min(r + NORM_ROWS, tm)) for r in range(0, tm, NORM_ROWS)]

    @pl.when(j == 0)
    def _():
        for rows in row_chunks:
            h = h_ref[rows, :]
            n_ref[rows, :] = _rmsnorm(h, g_ref[...]).astype(BF16)
            y_ref[rows, :] = h

    gate = jnp.dot(n_ref[...], wg_ref[...], preferred_element_type=F32)
    up = jnp.dot(n_ref[...], wu_ref[...], preferred_element_type=F32)
    hid = (gate * _sigmoid(gate) * up).astype(BF16)
    y_ref[...] += jnp.dot(hid, wd_ref[...], preferred_element_type=F32)

    @pl.when(j == pl.num_programs(1) - 1)
    def _():
        for rows in row_chunks:
            y_ref[rows, :] = _rmsnorm(y_ref[rows, :], gf_ref[...])


def _ffn(h, g_ffn, w_gate_up, w_down, g_final, *, tm):
    n = h.shape[0]
    tn = WEIGHT_TILE
    nj = D_FF // tn
    return pl.pallas_call(
        _ffn_kernel,
        grid=(n // tm, nj),
        in_specs=[
            pl.BlockSpec((tm, D_MODEL), lambda i, j: (i, 0)),
            pl.BlockSpec((1, D_MODEL), lambda i, j: (0, 0)),
            pl.BlockSpec((D_MODEL, tn), lambda i, j: (0, j)),
            pl.BlockSpec((D_MODEL, tn), lambda i, j: (0, nj + j)),
            pl.BlockSpec((tn, D_MODEL), lambda i, j: (j, 0)),
            pl.BlockSpec((1, D_MODEL), lambda i, j: (0, 0)),
        ],
        out_specs=pl.BlockSpec((tm, D_MODEL), lambda i, j: (i, 0)),
        out_shape=jax.ShapeDtypeStruct((n, D_MODEL), F32),
        scratch_shapes=[pltpu.VMEM((tm, D_MODEL), BF16)],
        compiler_params=_params("arbitrary", "arbitrary", vmem_limit=FFN_VMEM_LIMIT),
        name="ffn",
    )(h, g_ffn, w_gate_up, w_gate_up, w_down, g_final)


def _layer(x, hist, past_kv, weights, g_final, *, batch, seq, pos0):
    g_mix, w_in, w_pool, s_pool, w_a, w_b, w_out, g_ffn, w_gate_up, w_down = weights
    n = batch * seq
    tm = min(n, ROW_BLOCK)
    u, q, k, v, gates = _in_proj(x, g_mix, w_in, tm=tm)
    o_a = _pool(u, hist, w_pool, s_pool, batch=batch, seq=seq, tm=min(seq, POOL_ROWS), pos0=pos0)
    if past_kv is None:
        o_b = _attn_prompt(q, k, v, batch=batch, seq=seq)
    else:
        o_b = _attn_decode(q, k, v, past_kv[0], past_kv[1], batch=batch, rows=seq, past=pos0)
    h = _mix_out(o_a, o_b, gates, x, w_a, w_b, w_out, tm=min(n, MIX_ROWS))
    y = _ffn(h, g_ffn, w_gate_up, w_down, g_final, tm=tm)
    return y, u, k, v


def kernel(x_prompt, x_sample, cache_k, cache_v, state_pool, g_mix, w_in, w_pool, s_pool,
           w_branch, w_out, g_ffn, w_gate_up, w_down, g_final):
    depth = w_in.shape[0]
    assert depth == 1
    b_p, t_p, _ = x_prompt.shape
    b_s, t_s, _ = x_sample.shape
    past = cache_k.shape[2]
    l = 0
    weights = (
        g_mix[l][None, :], w_in[l].astype(BF16), w_pool[l].astype(BF16), s_pool[l][None, :],
        w_branch[l, :POOL_WIDTH].astype(BF16), w_branch[l, POOL_WIDTH:].astype(BF16),
        w_out[l].astype(BF16), g_ffn[l][None, :], w_gate_up[l].astype(BF16),
        w_down[l].astype(BF16),
    )
    g_fin = g_final[None, :]

    hist_p = jnp.zeros((b_p, HIST_ROWS, POOL_WIDTH), F32)
    y_p, u_p, k_p, v_p = _layer(
        x_prompt.reshape(b_p * t_p, D_MODEL), hist_p, None, weights, g_fin,
        batch=b_p, seq=t_p, pos0=0)

    hist_s = jnp.pad(state_pool[l], ((0, 0), (HIST_ROWS - POOL_HIST, 0), (0, 0)))
    past_kv = (cache_k[l].reshape(b_s * past * N_HEADS, HEAD_DIM),
               cache_v[l].reshape(b_s * past * N_HEADS, HEAD_DIM))
    y_s, u_s, k_s, v_s = _layer(
        x_sample.reshape(b_s * t_s, D_MODEL), hist_s, past_kv, weights, g_fin,
        batch=b_s, seq=t_s, pos0=past)

    new_pool_p = u_p.reshape(b_p, t_p, POOL_WIDTH)[:, t_p - POOL_HIST:]
    new_pool_s = jnp.concatenate([state_pool[l], u_s.reshape(b_s, t_s, POOL_WIDTH)],
                                 axis=1)[:, -POOL_HIST:]
    return (
        y_p.reshape(b_p, t_p, D_MODEL),
        y_s.reshape(b_s, t_s, D_MODEL),
        k_p.reshape(1, b_p, t_p, N_HEADS, HEAD_DIM),
        v_p.reshape(1, b_p, t_p, N_HEADS, HEAD_DIM),
        new_pool_p[None],
        k_s.reshape(1, b_s, t_s, N_HEADS, HEAD_DIM),
        v_s.reshape(1, b_s, t_s, N_HEADS, HEAD_DIM),
        new_pool_s[None],
    )
```

```python
import functools

import jax
import jax.numpy as jnp
from jax import lax
from jax.experimental import pallas as pl
from jax.experimental.pallas import tpu as pltpu

F32 = jnp.float32
BF16 = jnp.bfloat16

D_MODEL = 2048
N_HEADS = 8
HEAD_DIM = 128
ATTN_WIDTH = N_HEADS * HEAD_DIM
POOL_WINDOWS = (2, 4, 8, 16)
POOL_WIDTH = D_MODEL // 2
POOL_GROUP_WIDTH = POOL_WIDTH // len(POOL_WINDOWS)
POOL_HIST = max(POOL_WINDOWS) - 1
HIST_ROWS = POOL_HIST + 1
GROUP_WIDTH = 1024
N_GATE_GROUPS = 2 * D_MODEL // GROUP_WIDTH
D_FF = 5632
EPS = 1e-6

LANES = 128
CHUNK = 2 * LANES
Q_SCALE = 1.4426950408889634 * HEAD_DIM ** -0.5
VMEM_LIMIT = 56 * 1024 * 1024
FFN_VMEM_LIMIT = 60 * 1024 * 1024

ROW_BLOCK = 1024
WEIGHT_TILE = 512
Q_TILE = 512
DECODE_KEYS = 2048
MIX_ROWS = 512
MIX_COLS = 512
POOL_ROWS = 1024
NORM_ROWS = 256


def _params(*sem, vmem_limit=VMEM_LIMIT):
    return pltpu.CompilerParams(dimension_semantics=sem, vmem_limit_bytes=vmem_limit)


def _rmsnorm(x, g):
    ms = jnp.mean(x * x, axis=-1, keepdims=True)
    return x * lax.rsqrt(ms + EPS) * g


def _sigmoid(x):
    return 1.0 / (1.0 + jnp.exp(-x))


def _inproj_kernel(x_ref, g_ref, w_ref, u_ref, q_ref, k_ref, v_ref, gate_ref, xn_ref, *, tpg):
    j = pl.program_id(1)

    @pl.when(j == 0)
    def _():
        xn_ref[...] = _rmsnorm(x_ref[...], g_ref[...]).astype(BF16)

    def proj():
        return jnp.dot(xn_ref[...], w_ref[...], preferred_element_type=F32)

    @pl.when(j < tpg)
    def _():
        u_ref[...] = proj()

    @pl.when((j >= tpg) & (j < 2 * tpg))
    def _():
        q_ref[...] = (proj() * Q_SCALE).astype(BF16)

    @pl.when((j >= 2 * tpg) & (j < 3 * tpg))
    def _():
        k_ref[...] = proj()

    @pl.when((j >= 3 * tpg) & (j < 4 * tpg))
    def _():
        v_ref[...] = proj()

    @pl.when(j >= 4 * tpg)
    def _():
        gate_ref[...] = proj()


def _in_proj(x, g_mix, w_in, *, tm):
    n = x.shape[0]
    tn = WEIGHT_TILE
    tpg = GROUP_WIDTH // tn
    n_tiles = w_in.shape[1] // tn
    n_blocks = n // tm

    def group_spec(first, ntiles):
        def index(i, j):
            moved_on = (j >= first + ntiles) & (i + 1 < n_blocks)
            return (jnp.where(moved_on, i + 1, i),
                    jnp.where(moved_on, 0, jnp.clip(j - first, 0, ntiles - 1)))

        return pl.BlockSpec((tm, tn), index)

    def group_shape(dtype):
        return jax.ShapeDtypeStruct((n, GROUP_WIDTH), dtype)

    return pl.pallas_call(
        functools.partial(_inproj_kernel, tpg=tpg),
        grid=(n_blocks, n_tiles),
        in_specs=[
            pl.BlockSpec((tm, D_MODEL),
                         lambda i, j: (jnp.where(j > 0, jnp.minimum(i + 1, n_blocks - 1), i), 0)),
            pl.BlockSpec((1, D_MODEL), lambda i, j: (0, 0)),
            pl.BlockSpec((D_MODEL, tn), lambda i, j: (0, j)),
        ],
        out_specs=[group_spec(0, tpg), group_spec(tpg, tpg), group_spec(2 * tpg, tpg),
                   group_spec(3 * tpg, tpg), group_spec(4 * tpg, N_GATE_GROUPS * tpg)],
        out_shape=[group_shape(F32), group_shape(BF16), group_shape(F32), group_shape(F32),
                   jax.ShapeDtypeStruct((n, N_GATE_GROUPS * GROUP_WIDTH), F32)],
        scratch_shapes=[pltpu.VMEM((tm, D_MODEL), BF16)],
        compiler_params=_params("arbitrary", "arbitrary"),
        name="in_proj",
    )(x, g_mix, w_in)


def _pool_kernel(u_ref, uprev_ref, hist_ref, wp_ref, sp_ref, o_ref, buf_ref, *, tm, pos0):
    i = pl.program_id(1)
    buf_ref[HIST_ROWS:, :] = u_ref[...]

    @pl.when(i == 0)
    def _():
        buf_ref[:HIST_ROWS, :] = hist_ref[0]

    @pl.when(i > 0)
    def _():
        buf_ref[:HIST_ROWS, :] = uprev_ref[...]

    pos = pos0 + i * tm + lax.broadcasted_iota(jnp.int32, (tm, 1), 0)
    for g, w in enumerate(POOL_WINDOWS):
        cols = slice(g * POOL_GROUP_WIDTH, (g + 1) * POOL_GROUP_WIDTH)
        x = buf_ref[:, cols]
        s, m = x, 1
        while m < w:
            s = s + pltpu.roll(s, m, axis=0)
            m *= 2
        cur = x[HIST_ROWS:]
        cnt = jnp.minimum(pos + 1, w).astype(F32)
        diff = s[HIST_ROWS:] / cnt - cur
        o = jnp.dot(diff.astype(BF16), wp_ref[g], preferred_element_type=F32)
        o_ref[:, cols] = (o * sp_ref[:, cols]).astype(BF16)


def _pool(u, hist, w_pool, s_pool, *, batch, seq, tm, pos0):
    assert all(w & (w - 1) == 0 for w in POOL_WINDOWS)
    nt = seq // tm
    per = tm // HIST_ROWS
    return pl.pallas_call(
        functools.partial(_pool_kernel, tm=tm, pos0=pos0),
        grid=(batch, nt),
        in_specs=[
            pl.BlockSpec((tm, POOL_WIDTH), lambda b, i: (b * nt + i, 0)),
            pl.BlockSpec((HIST_ROWS, POOL_WIDTH),
                         lambda b, i: (jnp.maximum((b * nt + i) * per - 1, 0), 0)),
            pl.BlockSpec((1, HIST_ROWS, POOL_WIDTH), lambda b, i: (b, 0, 0)),
            pl.BlockSpec(w_pool.shape, lambda b, i: (0, 0, 0)),
            pl.BlockSpec((1, POOL_WIDTH), lambda b, i: (0, 0)),
        ],
        out_specs=pl.BlockSpec((tm, POOL_WIDTH), lambda b, i: (b * nt + i, 0)),
        out_shape=jax.ShapeDtypeStruct((batch * seq, POOL_WIDTH), BF16),
        scratch_shapes=[pltpu.VMEM((HIST_ROWS + tm, POOL_WIDTH), F32)],
        compiler_params=_params("arbitrary", "arbitrary"),
        name="pool",
    )(u, u, hist, w_pool, s_pool)


def _suffix_matrix():
    r = lax.broadcasted_iota(jnp.int32, (CHUNK, CHUNK), 0)
    c = lax.broadcasted_iota(jnp.int32, (CHUNK, CHUNK), 1)
    return jnp.where(r >= c, 1.0, 0.0).astype(BF16)


def _scores(q, kblk):
    return lax.dot_general(q, kblk, (((1,), (1,)), ((), ())), preferred_element_type=F32)


def _sb_weights(z, carry, sfx, *, stack_chunks):
    rows, span = z.shape
    n_chunks = span // CHUNK
    sp = (jnp.maximum(z, 0.0) + jnp.log2(1.0 + jnp.exp2(-jnp.abs(z)))).astype(BF16)

    def chunk(x, c):
        return x[:, c * CHUNK:(c + 1) * CHUNK]

    if stack_chunks:
        r_all = jnp.dot(jnp.concatenate([chunk(sp, c) for c in range(n_chunks)], axis=0), sfx,
                        preferred_element_type=F32)
    ws = []
    for c in reversed(range(n_chunks)):
        if stack_chunks:
            r = r_all[c * rows:(c + 1) * rows]
        else:
            r = jnp.dot(chunk(sp, c), sfx, preferred_element_type=F32)
        w = jnp.exp2(chunk(z, c) - (r + jnp.concatenate([carry] * (CHUNK // LANES), axis=1)))
        ws.insert(0, w.astype(BF16))
        carry = carry + jnp.broadcast_to(r[:, :1], (rows, LANES))
    return carry, jnp.concatenate(ws, axis=1)


def _causal_bias(rows, span):
    r = lax.broadcasted_iota(jnp.int32, (rows, span), 0)
    c = lax.broadcasted_iota(jnp.int32, (rows, span), 1)
    return jnp.where(c < r, 0.0, MASKED_LOGIT)


ATTN_UNROLL = 2
MASKED_LOGIT = -1e9


def _attn_prompt_kernel(q_ref, k_ref, v_ref, o_ref, kb_ref, vt_ref, sfx_ref, bias_ref,
                        z_ref, w_ref, carry_ref, acct_ref, *, seq):
    tile = Q_TILE
    nq = seq // tile
    kb_ref[...] = k_ref[...].astype(BF16)
    for j in range(nq):
        vt_ref[j] = v_ref[j * tile:(j + 1) * tile, :].T.astype(BF16)

    @pl.when((pl.program_id(0) == 0) & (pl.program_id(1) == 0))
    def _():
        sfx_ref[...] = _suffix_matrix()
        bias_ref[...] = _causal_bias(tile, tile)

    def rows(ref, i):
        return ref[pl.ds(pl.multiple_of(i * tile, tile), tile), :]

    def logits(t, slot):
        qi, j = jnp.minimum(t[0], nq - 1), jnp.minimum(t[1], nq - 1)
        z_ref[slot] = _scores(rows(q_ref, qi), rows(kb_ref, j))

    def weights(t, slot, diagonal):
        qi, _ = t
        if not diagonal:
            carry, w = _sb_weights(z_ref[slot], carry_ref[qi], sfx_ref[...], stack_chunks=False)
            carry_ref[qi] = carry
            w_ref[slot] = w
            return
        half = tile // 2
        for r, span in ((slice(0, half), half), (slice(half, tile), tile)):
            carry, w = _sb_weights(z_ref[slot, r, :span] + bias_ref[r, :span],
                                   jnp.zeros((half, LANES), F32), sfx_ref[...], stack_chunks=False)
            carry_ref[qi, r, :] = carry
            w_ref[slot, r, :span] = w
        w_ref[slot, :half, half:] = jnp.zeros((half, half), BF16)

    def value_product(t, slot, store):
        qi, j = t
        pv = lax.dot_general(vt_ref[j], w_ref[slot], (((1,), (1,)), ((), ())),
                             preferred_element_type=F32)
        acct_ref[qi] = pv if store else acct_ref[qi] + pv

    def walk(n_steps, following, diagonal, st):
        assert n_steps % ATTN_UNROLL == 0 and ATTN_UNROLL % 2 == 0

        def group(i, st):
            prev, cur = (st[0], st[1]), (st[2], st[3])
            for s in range(ATTN_UNROLL):
                slot = s % 2
                nxt = following(cur)
                logits(nxt, 1 - slot)
                weights(cur, slot, diagonal)
                value_product(prev, 1 - slot, store=diagonal)
                prev, cur = cur, nxt
            return prev + cur

        return lax.fori_loop(0, n_steps // ATTN_UNROLL, group, st)

    def next_diagonal(t):
        last = t[0] == nq - 1
        return jnp.where(last, 1, t[0] + 1), jnp.where(last, 0, t[1] + 1)

    def next_below(t):
        qi, j = t
        wrap = j == 0
        return jnp.where(wrap, qi + 1, qi), jnp.where(wrap, qi, j - 1)

    zero = jnp.int32(0)
    acct_ref[nq - 1] = jnp.zeros((HEAD_DIM, tile), F32)
    w_ref[1] = jnp.zeros((tile, tile), BF16)
    logits((zero, zero), 0)
    st = walk(nq, next_diagonal, True, (zero, zero, zero, zero))
    st = walk(nq * (nq - 1) // 2, next_below, False, st)
    value_product((st[0], st[1]), 1, store=False)
    for qi in range(nq):
        o_ref[qi * tile:(qi + 1) * tile, :] = acct_ref[qi].T.astype(BF16)


def _attn_prompt(q, k, v, *, batch, seq):
    assert seq % Q_TILE == 0
    spec = pl.BlockSpec((seq, HEAD_DIM), lambda b, h: (b, h))
    return pl.pallas_call(
        functools.partial(_attn_prompt_kernel, seq=seq),
        grid=(batch, N_HEADS),
        in_specs=[spec, spec, spec],
        out_specs=spec,
        out_shape=jax.ShapeDtypeStruct((batch * seq, ATTN_WIDTH), BF16),
        scratch_shapes=[
            pltpu.VMEM((seq, HEAD_DIM), BF16),
            pltpu.VMEM((seq // Q_TILE, HEAD_DIM, Q_TILE), BF16),
            pltpu.VMEM((CHUNK, CHUNK), BF16),
            pltpu.VMEM((Q_TILE, Q_TILE), F32),
            pltpu.VMEM((2, Q_TILE, Q_TILE), F32),
            pltpu.VMEM((2, Q_TILE, Q_TILE), BF16),
            pltpu.VMEM((seq // Q_TILE, Q_TILE, LANES), F32),
            pltpu.VMEM((seq // Q_TILE, HEAD_DIM, Q_TILE), F32),
        ],
        compiler_params=_params("arbitrary", "arbitrary"),
        name="attn_prompt",
    )(q, k, v)


def _attn_decode_kernel(q_ref, kn_ref, vn_ref, kp_ref, vp_ref, o_ref, sfx_ref, carry_ref, acc_ref,
                        *, rows):
    c = pl.program_id(1)

    def head_cols(h):
        return slice(h * HEAD_DIM, (h + 1) * HEAD_DIM)

    @pl.when(c == 0)
    def _():
        sfx_ref[...] = _suffix_matrix()
        bias = _causal_bias(rows, CHUNK)
        pad = jnp.zeros((CHUNK - rows, HEAD_DIM), BF16)
        for h in range(N_HEADS):
            q = q_ref[:, head_cols(h)]
            kn = jnp.concatenate([kn_ref[:, head_cols(h)].astype(BF16), pad], axis=0)
            vn = jnp.concatenate([vn_ref[:, head_cols(h)].astype(BF16), pad], axis=0)
            carry, w = _sb_weights(_scores(q, kn) + bias, jnp.zeros((rows, LANES), F32),
                                   sfx_ref[...], stack_chunks=True)
            carry_ref[h] = carry
            acc_ref[h] = jnp.dot(w, vn, preferred_element_type=F32)

    for h in range(N_HEADS):
        q = q_ref[:, head_cols(h)]
        k = kp_ref[pl.ds(h, DECODE_KEYS, stride=N_HEADS), :].astype(BF16)
        v = vp_ref[pl.ds(h, DECODE_KEYS, stride=N_HEADS), :].astype(BF16)
        carry, w = _sb_weights(_scores(q, k), carry_ref[h], sfx_ref[...], stack_chunks=True)
        carry_ref[h] = carry
        acc_ref[h] += jnp.dot(w, v, preferred_element_type=F32)

    @pl.when(c == pl.num_programs(1) - 1)
    def _():
        for h in range(N_HEADS):
            o_ref[:, head_cols(h)] = acc_ref[h].astype(BF16)


def _attn_decode(q, k_new, v_new, k_past, v_past, *, batch, rows, past):
    assert rows <= LANES and past % DECODE_KEYS == 0
    n_chunks = past // DECODE_KEYS
    new_spec = pl.BlockSpec((rows, ATTN_WIDTH), lambda b, c: (b, 0))
    past_spec = pl.BlockSpec((DECODE_KEYS * N_HEADS, HEAD_DIM),
                             lambda b, c: (b * n_chunks + n_chunks - 1 - c, 0))
    return pl.pallas_call(
        functools.partial(_attn_decode_kernel, rows=rows),
        grid=(batch, n_chunks),
        in_specs=[new_spec, new_spec, new_spec, past_spec, past_spec],
        out_specs=new_spec,
        out_shape=jax.ShapeDtypeStruct((batch * rows, ATTN_WIDTH), BF16),
        scratch_shapes=[pltpu.VMEM((CHUNK, CHUNK), BF16),
                        pltpu.VMEM((N_HEADS, rows, LANES), F32),
                        pltpu.VMEM((N_HEADS, rows, HEAD_DIM), F32)],
        compiler_params=_params("arbitrary", "arbitrary"),
        name="attn_decode",
    )(q, k_new, v_new, k_past, v_past)


def _mix_out_kernel(oa_ref, ob_ref, gates_ref, x_ref, wa_ref, wb_ref, wo_ref, h_ref, m_ref):
    chunks = [slice(c * MIX_COLS, (c + 1) * MIX_COLS) for c in range(D_MODEL // MIX_COLS)]
    for cols in chunks:
        ya = jnp.dot(oa_ref[...], wa_ref[:, cols], preferred_element_type=F32)
        yb = jnp.dot(ob_ref[...], wb_ref[:, cols], preferred_element_type=F32)
        gate_b_cols = slice(D_MODEL + cols.start, D_MODEL + cols.stop)
        m_ref[:, cols] = (_sigmoid(gates_ref[:, cols]) * ya
                          + _sigmoid(gates_ref[:, gate_b_cols]) * yb).astype(BF16)
    for cols in chunks:
        h_ref[:, cols] = x_ref[:, cols] + jnp.dot(m_ref[...], wo_ref[:, cols],
                                                  preferred_element_type=F32)


def _mix_out(o_a, o_b, gates, x, w_a, w_b, w_out, *, tm):
    n = x.shape[0]

    def rows(width):
        return pl.BlockSpec((tm, width), lambda i: (i, 0))

    def resident(w):
        return pl.BlockSpec(w.shape, lambda i: (0, 0), pipeline_mode=pl.Buffered(1))

    return pl.pallas_call(
        _mix_out_kernel,
        grid=(n // tm,),
        in_specs=[rows(POOL_WIDTH), rows(ATTN_WIDTH), rows(2 * D_MODEL), rows(D_MODEL),
                  resident(w_a), resident(w_b), resident(w_out)],
        out_specs=rows(D_MODEL),
        out_shape=jax.ShapeDtypeStruct((n, D_MODEL), F32),
        scratch_shapes=[pltpu.VMEM((tm, D_MODEL), BF16)],
        compiler_params=_params("arbitrary"),
        name="mix_out",
    )(o_a, o_b, gates, x, w_a, w_b, w_out)


def _ffn_kernel(h_ref, g_ref, wg_ref, wu_ref, wd_ref, gf_ref, y_ref, n_ref):
    j = pl.program_id(1)
    tm = h_ref.shape[0]
    row_chunks = [slice(r, min(r + NORM_ROWS, tm)) for r in range(0, tm, NORM_ROWS)]

    @pl.when(j == 0)
    def _():
        for rows in row_chunks:
            h = h_ref[rows, :]
            n_ref[rows, :] = _rmsnorm(h, g_ref[...]).astype(BF16)
            y_ref[rows, :] = h

    gate = jnp.dot(n_ref[...], wg_ref[...], preferred_element_type=F32)
    up = jnp.dot(n_ref[...], wu_ref[...], preferred_element_type=F32)
    hid = (gate * _sigmoid(gate) * up).astype(BF16)
    y_ref[...] += jnp.dot(hid, wd_ref[...], preferred_element_type=F32)

    @pl.when(j == pl.num_programs(1) - 1)
    def _():
        for rows in row_chunks:
            y_ref[rows, :] = _rmsnorm(y_ref[rows, :], gf_ref[...])


def _ffn(h, g_ffn, w_gate_up, w_down, g_final, *, tm):
    n = h.shape[0]
    tn = WEIGHT_TILE
    nj = D_FF // tn
    return pl.pallas_call(
        _ffn_kernel,
        grid=(n // tm, nj),
        in_specs=[
            pl.BlockSpec((tm, D_MODEL), lambda i, j: (i, 0)),
            pl.BlockSpec((1, D_MODEL), lambda i, j: (0, 0)),
            pl.BlockSpec((D_MODEL, tn), lambda i, j: (0, j)),
            pl.BlockSpec((D_MODEL, tn), lambda i, j: (0, nj + j)),
            pl.BlockSpec((tn, D_MODEL), lambda i, j: (j, 0)),
            pl.BlockSpec((1, D_MODEL), lambda i, j: (0, 0)),
        ],
        out_specs=pl.BlockSpec((tm, D_MODEL), lambda i, j: (i, 0)),
        out_shape=jax.ShapeDtypeStruct((n, D_MODEL), F32),
        scratch_shapes=[pltpu.VMEM((tm, D_MODEL), BF16)],
        compiler_params=_params("arbitrary", "arbitrary", vmem_limit=FFN_VMEM_LIMIT),
        name="ffn",
    )(h, g_ffn, w_gate_up, w_gate_up, w_down, g_final)


def _layer(x, hist, past_kv, weights, g_final, *, batch, seq, pos0):
    g_mix, w_in, w_pool, s_pool, w_a, w_b, w_out, g_ffn, w_gate_up, w_down = weights
    n = batch * seq
    tm = min(n, ROW_BLOCK)
    u, q, k, v, gates = _in_proj(x, g_mix, w_in, tm=tm)
    o_a = _pool(u, hist, w_pool, s_pool, batch=batch, seq=seq, tm=min(seq, POOL_ROWS), pos0=pos0)
    if past_kv is None:
        o_b = _attn_prompt(q, k, v, batch=batch, seq=seq)
    else:
        o_b = _attn_decode(q, k, v, past_kv[0], past_kv[1], batch=batch, rows=seq, past=pos0)
    h = _mix_out(o_a, o_b, gates, x, w_a, w_b, w_out, tm=min(n, MIX_ROWS))
    y = _ffn(h, g_ffn, w_gate_up, w_down, g_final, tm=tm)
    return y, u, k, v


def kernel(x_prompt, x_sample, cache_k, cache_v, state_pool, g_mix, w_in, w_pool, s_pool,
           w_branch, w_out, g_ffn, w_gate_up, w_down, g_final):
    depth = w_in.shape[0]
    assert depth == 1
    b_p, t_p, _ = x_prompt.shape
    b_s, t_s, _ = x_sample.shape
    past = cache_k.shape[2]
    l = 0
    weights = (
        g_mix[l][None, :], w_in[l].astype(BF16), w_pool[l].astype(BF16), s_pool[l][None, :],
        w_branch[l, :POOL_WIDTH].astype(BF16), w_branch[l, POOL_WIDTH:].astype(BF16),
        w_out[l].astype(BF16), g_ffn[l][None, :], w_gate_up[l].astype(BF16),
        w_down[l].astype(BF16),
    )
    g_fin = g_final[None, :]

    hist_p = jnp.zeros((b_p, HIST_ROWS, POOL_WIDTH), F32)
    y_p, u_p, k_p, v_p = _layer(
        x_prompt.reshape(b_p * t_p, D_MODEL), hist_p, None, weights, g_fin,
        batch=b_p, seq=t_p, pos0=0)

    hist_s = jnp.pad(state_pool[l], ((0, 0), (HIST_ROWS - POOL_HIST, 0), (0, 0)))
    past_kv = (cache_k[l].reshape(b_s * past * N_HEADS, HEAD_DIM),
               cache_v[l].reshape(b_s * past * N_HEADS, HEAD_DIM))
    y_s, u_s, k_s, v_s = _layer(
        x_sample.reshape(b_s * t_s, D_MODEL), hist_s, past_kv, weights, g_fin,
        batch=b_s, seq=t_s, pos0=past)

    new_pool_p = u_p.reshape(b_p, t_p, POOL_WIDTH)[:, t_p - POOL_HIST:]
    new_pool_s = jnp.concatenate([state_pool[l], u_s.reshape(b_s, t_s, POOL_WIDTH)],
                                 axis=1)[:, -POOL_HIST:]
    return (
        y_p.reshape(b_p, t_p, D_MODEL),
        y_s.reshape(b_s, t_s, D_MODEL),
        k_p.reshape(1, b_p, t_p, N_HEADS, HEAD_DIM),
        v_p.reshape(1, b_p, t_p, N_HEADS, HEAD_DIM),
        new_pool_p[None],
        k_s.reshape(1, b_s, t_s, N_HEADS, HEAD_DIM),
        v_s.reshape(1, b_s, t_s, N_HEADS, HEAD_DIM),
        new_pool_s[None],
    )
```

```python
import functools

import jax
import jax.numpy as jnp
from jax import lax
from jax.experimental import pallas as pl
from jax.experimental.pallas import tpu as pltpu

F32 = jnp.float32
BF16 = jnp.bfloat16

D_MODEL = 2048
N_HEADS = 8
HEAD_DIM = 128
ATTN_WIDTH = N_HEADS * HEAD_DIM
POOL_WINDOWS = (2, 4, 8, 16)
POOL_WIDTH = D_MODEL // 2
POOL_GROUP_WIDTH = POOL_WIDTH // len(POOL_WINDOWS)
POOL_HIST = max(POOL_WINDOWS) - 1
HIST_ROWS = POOL_HIST + 1
GROUP_WIDTH = 1024
N_GATE_GROUPS = 2 * D_MODEL // GROUP_WIDTH
D_FF = 5632
EPS = 1e-6

LANES = 128
CHUNK = 2 * LANES
Q_SCALE = 1.4426950408889634 * HEAD_DIM ** -0.5
VMEM_LIMIT = 56 * 1024 * 1024
FFN_VMEM_LIMIT = 60 * 1024 * 1024

ROW_BLOCK = 1024
WEIGHT_TILE = 512
Q_TILE = 512
DECODE_KEYS = 2048
MIX_ROWS = 512
MIX_COLS = 512
POOL_ROWS = 1024
NORM_ROWS = 256


def _params(*sem, vmem_limit=VMEM_LIMIT):
    return pltpu.CompilerParams(dimension_semantics=sem, vmem_limit_bytes=vmem_limit)


def _rmsnorm(x, g):
    ms = jnp.mean(x * x, axis=-1, keepdims=True)
    return x * lax.rsqrt(ms + EPS) * g


def _sigmoid(x):
    return 1.0 / (1.0 + jnp.exp(-x))


def _inproj_kernel(x_ref, g_ref, w_ref, u_ref, q_ref, k_ref, v_ref, gate_ref, xn_ref, *, tpg):
    j = pl.program_id(1)

    @pl.when(j == 0)
    def _():
        xn_ref[...] = _rmsnorm(x_ref[...], g_ref[...]).astype(BF16)

    def proj():
        return jnp.dot(xn_ref[...], w_ref[...], preferred_element_type=F32)

    @pl.when(j < tpg)
    def _():
        u_ref[...] = proj()

    @pl.when((j >= tpg) & (j < 2 * tpg))
    def _():
        q_ref[...] = (proj() * Q_SCALE).astype(BF16)

    @pl.when((j >= 2 * tpg) & (j < 3 * tpg))
    def _():
        k_ref[...] = proj()

    @pl.when((j >= 3 * tpg) & (j < 4 * tpg))
    def _():
        v_ref[...] = proj()

    @pl.when(j >= 4 * tpg)
    def _():
        gate_ref[...] = proj()


def _in_proj(x, g_mix, w_in, *, tm):
    n = x.shape[0]
    tn = WEIGHT_TILE
    tpg = GROUP_WIDTH // tn
    n_tiles = w_in.shape[1] // tn
    n_blocks = n // tm

    def group_spec(first, ntiles):
        def index(i, j):
            moved_on = (j >= first + ntiles) & (i + 1 < n_blocks)
            return (jnp.where(moved_on, i + 1, i),
                    jnp.where(moved_on, 0, jnp.clip(j - first, 0, ntiles - 1)))

        return pl.BlockSpec((tm, tn), index)

    def group_shape(dtype):
        return jax.ShapeDtypeStruct((n, GROUP_WIDTH), dtype)

    return pl.pallas_call(
        functools.partial(_inproj_kernel, tpg=tpg),
        grid=(n_blocks, n_tiles),
        in_specs=[
            pl.BlockSpec((tm, D_MODEL),
                         lambda i, j: (jnp.where(j > 0, jnp.minimum(i + 1, n_blocks - 1), i), 0)),
            pl.BlockSpec((1, D_MODEL), lambda i, j: (0, 0)),
            pl.BlockSpec((D_MODEL, tn), lambda i, j: (0, j)),
        ],
        out_specs=[group_spec(0, tpg), group_spec(tpg, tpg), group_spec(2 * tpg, tpg),
                   group_spec(3 * tpg, tpg), group_spec(4 * tpg, N_GATE_GROUPS * tpg)],
        out_shape=[group_shape(F32), group_shape(BF16), group_shape(F32), group_shape(F32),
                   jax.ShapeDtypeStruct((n, N_GATE_GROUPS * GROUP_WIDTH), F32)],
        scratch_shapes=[pltpu.VMEM((tm, D_MODEL), BF16)],
        compiler_params=_params("arbitrary", "arbitrary"),
        name="in_proj",
    )(x, g_mix, w_in)


def _pool_kernel(u_ref, uprev_ref, hist_ref, wp_ref, sp_ref, o_ref, buf_ref, *, tm, pos0):
    i = pl.program_id(1)
    buf_ref[HIST_ROWS:, :] = u_ref[...]

    @pl.when(i == 0)
    def _():
        buf_ref[:HIST_ROWS, :] = hist_ref[0]

    @pl.when(i > 0)
    def _():
        buf_ref[:HIST_ROWS, :] = uprev_ref[...]

    pos = pos0 + i * tm + lax.broadcasted_iota(jnp.int32, (tm, 1), 0)
    for g, w in enumerate(POOL_WINDOWS):
        cols = slice(g * POOL_GROUP_WIDTH, (g + 1) * POOL_GROUP_WIDTH)
        x = buf_ref[:, cols]
        s, m = x, 1
        while m < w:
            s = s + pltpu.roll(s, m, axis=0)
            m *= 2
        cur = x[HIST_ROWS:]
        cnt = jnp.minimum(pos + 1, w).astype(F32)
        diff = s[HIST_ROWS:] / cnt - cur
        o = jnp.dot(diff.astype(BF16), wp_ref[g], preferred_element_type=F32)
        o_ref[:, cols] = (o * sp_ref[:, cols]).astype(BF16)


def _pool(u, hist, w_pool, s_pool, *, batch, seq, tm, pos0):
    assert all(w & (w - 1) == 0 for w in POOL_WINDOWS)
    nt = seq // tm
    per = tm // HIST_ROWS
    return pl.pallas_call(
        functools.partial(_pool_kernel, tm=tm, pos0=pos0),
        grid=(batch, nt),
        in_specs=[
            pl.BlockSpec((tm, POOL_WIDTH), lambda b, i: (b * nt + i, 0)),
            pl.BlockSpec((HIST_ROWS, POOL_WIDTH),
                         lambda b, i: (jnp.maximum((b * nt + i) * per - 1, 0), 0)),
            pl.BlockSpec((1, HIST_ROWS, POOL_WIDTH), lambda b, i: (b, 0, 0)),
            pl.BlockSpec(w_pool.shape, lambda b, i: (0, 0, 0)),
            pl.BlockSpec((1, POOL_WIDTH), lambda b, i: (0, 0)),
        ],
        out_specs=pl.BlockSpec((tm, POOL_WIDTH), lambda b, i: (b * nt + i, 0)),
        out_shape=jax.ShapeDtypeStruct((batch * seq, POOL_WIDTH), BF16),
        scratch_shapes=[pltpu.VMEM((HIST_ROWS + tm, POOL_WIDTH), F32)],
        compiler_params=_params("arbitrary", "arbitrary"),
        name="pool",
    )(u, u, hist, w_pool, s_pool)


def _suffix_matrix():
    r = lax.broadcasted_iota(jnp.int32, (CHUNK, CHUNK), 0)
    c = lax.broadcasted_iota(jnp.int32, (CHUNK, CHUNK), 1)
    return jnp.where(r >= c, 1.0, 0.0).astype(BF16)


def _scores(q, kblk):
    return lax.dot_general(q, kblk, (((1,), (1,)), ((), ())), preferred_element_type=F32)


def _sb_weights(z, carry, sfx, *, stack_chunks):
    rows, span = z.shape
    n_chunks = span // CHUNK
    sp = (jnp.maximum(z, 0.0) + jnp.log2(1.0 + jnp.exp2(-jnp.abs(z)))).astype(BF16)

    def chunk(x, c):
        return x[:, c * CHUNK:(c + 1) * CHUNK]

    if stack_chunks:
        r_all = jnp.dot(jnp.concatenate([chunk(sp, c) for c in range(n_chunks)], axis=0), sfx,
                        preferred_element_type=F32)
    ws = []
    for c in reversed(range(n_chunks)):
        if stack_chunks:
            r = r_all[c * rows:(c + 1) * rows]
        else:
            r = jnp.dot(chunk(sp, c), sfx, preferred_element_type=F32)
        w = jnp.exp2(chunk(z, c) - (r + jnp.concatenate([carry] * (CHUNK // LANES), axis=1)))
        ws.insert(0, w.astype(BF16))
        carry = carry + jnp.broadcast_to(r[:, :1], (rows, LANES))
    return carry, jnp.concatenate(ws, axis=1)


def _causal_bias(rows, span):
    r = lax.broadcasted_iota(jnp.int32, (rows, span), 0)
    c = lax.broadcasted_iota(jnp.int32, (rows, span), 1)
    return jnp.where(c < r, 0.0, MASKED_LOGIT)


ATTN_UNROLL = 4
MASKED_LOGIT = -1e9


def _attn_prompt_kernel(q_ref, k_ref, v_ref, o_ref, kb_ref, vt_ref, sfx_ref, bias_ref,
                        z_ref, w_ref, carry_ref, acct_ref, *, seq):
    tile = Q_TILE
    nq = seq // tile
    kb_ref[...] = k_ref[...].astype(BF16)
    for j in range(nq):
        vt_ref[j] = v_ref[j * tile:(j + 1) * tile, :].T.astype(BF16)

    @pl.when((pl.program_id(0) == 0) & (pl.program_id(1) == 0))
    def _():
        sfx_ref[...] = _suffix_matrix()
        bias_ref[...] = _causal_bias(tile, tile)

    def rows(ref, i):
        return ref[pl.ds(pl.multiple_of(i * tile, tile), tile), :]

    def logits(t, slot):
        qi, j = jnp.minimum(t[0], nq - 1), jnp.minimum(t[1], nq - 1)
        z_ref[slot] = _scores(rows(q_ref, qi), rows(kb_ref, j))

    def weights(t, slot, diagonal):
        qi, _ = t
        if not diagonal:
            carry, w = _sb_weights(z_ref[slot], carry_ref[qi], sfx_ref[...], stack_chunks=False)
            carry_ref[qi] = carry
            w_ref[slot] = w
            return
        half = tile // 2
        for r, span in ((slice(0, half), half), (slice(half, tile), tile)):
            carry, w = _sb_weights(z_ref[slot, r, :span] + bias_ref[r, :span],
                                   jnp.zeros((half, LANES), F32), sfx_ref[...], stack_chunks=False)
            carry_ref[qi, r, :] = carry
            w_ref[slot, r, :span] = w
        w_ref[slot, :half, half:] = jnp.zeros((half, half), BF16)

    def value_product(t, slot, store):
        qi, j = t
        pv = lax.dot_general(vt_ref[j], w_ref[slot], (((1,), (1,)), ((), ())),
                             preferred_element_type=F32)
        acct_ref[qi] = pv if store else acct_ref[qi] + pv

    def walk(n_steps, following, diagonal, st):
        assert n_steps % ATTN_UNROLL == 0 and ATTN_UNROLL % 2 == 0

        def group(i, st):
            prev, cur = (st[0], st[1]), (st[2], st[3])
            for s in range(ATTN_UNROLL):
                slot = s % 2
                nxt = following(cur)
                logits(nxt, 1 - slot)
                weights(cur, slot, diagonal)
                value_product(prev, 1 - slot, store=diagonal)
                prev, cur = cur, nxt
            return prev + cur

        return lax.fori_loop(0, n_steps // ATTN_UNROLL, group, st)

    def next_diagonal(t):
        last = t[0] == nq - 1
        return jnp.where(last, 1, t[0] + 1), jnp.where(last, 0, t[1] + 1)

    def next_below(t):
        qi, j = t
        wrap = j == 0
        return jnp.where(wrap, qi + 1, qi), jnp.where(wrap, qi, j - 1)

    zero = jnp.int32(0)
    acct_ref[nq - 1] = jnp.zeros((HEAD_DIM, tile), F32)
    w_ref[1] = jnp.zeros((tile, tile), BF16)
    logits((zero, zero), 0)
    st = walk(nq, next_diagonal, True, (zero, zero, zero, zero))
    st = walk(nq * (nq - 1) // 2, next_below, False, st)
    value_product((st[0], st[1]), 1, store=False)
    for qi in range(nq):
        o_ref[qi * tile:(qi + 1) * tile, :] = acct_ref[qi].T.astype(BF16)


def _attn_prompt(q, k, v, *, batch, seq):
    assert seq % Q_TILE == 0
    spec = pl.BlockSpec((seq, HEAD_DIM), lambda b, h: (b, h))
    return pl.pallas_call(
        functools.partial(_attn_prompt_kernel, seq=seq),
        grid=(batch, N_HEADS),
        in_specs=[spec, spec, spec],
        out_specs=spec,
        out_shape=jax.ShapeDtypeStruct((batch * seq, ATTN_WIDTH), BF16),
        scratch_shapes=[
            pltpu.VMEM((seq, HEAD_DIM), BF16),
            pltpu.VMEM((seq // Q_TILE, HEAD_DIM, Q_TILE), BF16),
            pltpu.VMEM((CHUNK, CHUNK), BF16),
            pltpu.VMEM((Q_TILE, Q_TILE), F32),
            pltpu.VMEM((2, Q_TILE, Q_TILE), F32),
            pltpu.VMEM((2, Q_TILE, Q_TILE), BF16),
            pltpu.VMEM((seq // Q_TILE, Q_TILE, LANES), F32),
            pltpu.VMEM((seq // Q_TILE, HEAD_DIM, Q_TILE), F32),
        ],
        compiler_params=_params("arbitrary", "arbitrary"),
        name="attn_prompt",
    )(q, k, v)


def _attn_decode_kernel(q_ref, kn_ref, vn_ref, kp_ref, vp_ref, o_ref, sfx_ref, carry_ref, acc_ref,
                        *, rows):
    c = pl.program_id(1)

    def head_cols(h):
        return slice(h * HEAD_DIM, (h + 1) * HEAD_DIM)

    @pl.when(c == 0)
    def _():
        sfx_ref[...] = _suffix_matrix()
        bias = _causal_bias(rows, CHUNK)
        pad = jnp.zeros((CHUNK - rows, HEAD_DIM), BF16)
        for h in range(N_HEADS):
            q = q_ref[:, head_cols(h)]
            kn = jnp.concatenate([kn_ref[:, head_cols(h)].astype(BF16), pad], axis=0)
            vn = jnp.concatenate([vn_ref[:, head_cols(h)].astype(BF16), pad], axis=0)
            carry, w = _sb_weights(_scores(q, kn) + bias, jnp.zeros((rows, LANES), F32),
                                   sfx_ref[...], stack_chunks=True)
            carry_ref[h] = carry
            acc_ref[h] = jnp.dot(w, vn, preferred_element_type=F32)

    for h in range(N_HEADS):
        q = q_ref[:, head_cols(h)]
        k = kp_ref[pl.ds(h, DECODE_KEYS, stride=N_HEADS), :].astype(BF16)
        v = vp_ref[pl.ds(h, DECODE_KEYS, stride=N_HEADS), :].astype(BF16)
        carry, w = _sb_weights(_scores(q, k), carry_ref[h], sfx_ref[...], stack_chunks=True)
        carry_ref[h] = carry
        acc_ref[h] += jnp.dot(w, v, preferred_element_type=F32)

    @pl.when(c == pl.num_programs(1) - 1)
    def _():
        for h in range(N_HEADS):
            o_ref[:, head_cols(h)] = acc_ref[h].astype(BF16)


def _attn_decode(q, k_new, v_new, k_past, v_past, *, batch, rows, past):
    assert rows <= LANES and past % DECODE_KEYS == 0
    n_chunks = past // DECODE_KEYS
    new_spec = pl.BlockSpec((rows, ATTN_WIDTH), lambda b, c: (b, 0))
    past_spec = pl.BlockSpec((DECODE_KEYS * N_HEADS, HEAD_DIM),
                             lambda b, c: (b * n_chunks + n_chunks - 1 - c, 0))
    return pl.pallas_call(
        functools.partial(_attn_decode_kernel, rows=rows),
        grid=(batch, n_chunks),
        in_specs=[new_spec, new_spec, new_spec, past_spec, past_spec],
        out_specs=new_spec,
        out_shape=jax.ShapeDtypeStruct((batch * rows, ATTN_WIDTH), BF16),
        scratch_shapes=[pltpu.VMEM((CHUNK, CHUNK), BF16),
                        pltpu.VMEM((N_HEADS, rows, LANES), F32),
                        pltpu.VMEM((N_HEADS, rows, HEAD_DIM), F32)],
        compiler_params=_params("arbitrary", "arbitrary"),
        name="attn_decode",
    )(q, k_new, v_new, k_past, v_past)


def _mix_out_kernel(oa_ref, ob_ref, gates_ref, x_ref, wa_ref, wb_ref, wo_ref, h_ref, m_ref):
    chunks = [slice(c * MIX_COLS, (c + 1) * MIX_COLS) for c in range(D_MODEL // MIX_COLS)]
    for cols in chunks:
        ya = jnp.dot(oa_ref[...], wa_ref[:, cols], preferred_element_type=F32)
        yb = jnp.dot(ob_ref[...], wb_ref[:, cols], preferred_element_type=F32)
        gate_b_cols = slice(D_MODEL + cols.start, D_MODEL + cols.stop)
        m_ref[:, cols] = (_sigmoid(gates_ref[:, cols]) * ya
                          + _sigmoid(gates_ref[:, gate_b_cols]) * yb).astype(BF16)
    for cols in chunks:
        h_ref[:, cols] = x_ref[:, cols] + jnp.dot(m_ref[...], wo_ref[:, cols],
                                                  preferred_element_type=F32)


def _mix_out(o_a, o_b, gates, x, w_a, w_b, w_out, *, tm):
    n = x.shape[0]

    def rows(width):
        return pl.BlockSpec((tm, width), lambda i: (i, 0))

    def resident(w):
        return pl.BlockSpec(w.shape, lambda i: (0, 0), pipeline_mode=pl.Buffered(1))

    return pl.pallas_call(
        _mix_out_kernel,
        grid=(n // tm,),
        in_specs=[rows(POOL_WIDTH), rows(ATTN_WIDTH), rows(2 * D_MODEL), rows(D_MODEL),
                  resident(w_a), resident(w_b), resident(w_out)],
        out_specs=rows(D_MODEL),
        out_shape=jax.ShapeDtypeStruct((n, D_MODEL), F32),
        scratch_shapes=[pltpu.VMEM((tm, D_MODEL), BF16)],
        compiler_params=_params("arbitrary"),
        name="mix_out",
    )(o_a, o_b, gates, x, w_a, w_b, w_out)


def _ffn_kernel(h_ref, g_ref, wg_ref, wu_ref, wd_ref, gf_ref, y_ref, n_ref):
    j = pl.program_id(1)
    tm = h_ref.shape[0]
    row_chunks = [slice(r, min(r + NORM_ROWS, tm)) for r in range(0, tm, NORM_ROWS)]

    @pl.when(j == 0)
    def _():
        for rows in row_chunks:
            h = h_ref[rows, :]
            n_ref[rows, :] = _rmsnorm(h, g_ref[...]).astype(BF16)
            y_ref[rows, :] = h

    gate = jnp.dot(n_ref[...], wg_ref[...], preferred_element_type=F32)
    up = jnp.dot(n_ref[...], wu_ref[...], preferred_element_type=F32)
    hid = (gate * _sigmoid(gate) * up).astype(BF16)
    y_ref[...] += jnp.dot(hid, wd_ref[...], preferred_element_type=F32)

    @pl.when(j == pl.num_programs(1) - 1)
    def _():
        for rows in row_chunks:
            y_ref[rows, :] = _rmsnorm(y_ref[rows, :], gf_ref[...])


def _ffn(h, g_ffn, w_gate_up, w_down, g_final, *, tm):
    n = h.shape[0]
    tn = WEIGHT_TILE
    nj = D_FF // tn
    return pl.pallas_call(
        _ffn_kernel,
        grid=(n // tm, nj),
        in_specs=[
            pl.BlockSpec((tm, D_MODEL), lambda i, j: (i, 0)),
            pl.BlockSpec((1, D_MODEL), lambda i, j: (0, 0)),
            pl.BlockSpec((D_MODEL, tn), lambda i, j: (0, j)),
            pl.BlockSpec((D_MODEL, tn), lambda i, j: (0, nj + j)),
            pl.BlockSpec((tn, D_MODEL), lambda i, j: (j, 0)),
            pl.BlockSpec((1, D_MODEL), lambda i, j: (0, 0)),
        ],
        out_specs=pl.BlockSpec((tm, D_MODEL), lambda i, j: (i, 0)),
        out_shape=jax.ShapeDtypeStruct((n, D_MODEL), F32),
        scratch_shapes=[pltpu.VMEM((tm, D_MODEL), BF16)],
        compiler_params=_params("arbitrary", "arbitrary", vmem_limit=FFN_VMEM_LIMIT),
        name="ffn",
    )(h, g_ffn, w_gate_up, w_gate_up, w_down, g_final)


def _layer(x, hist, past_kv, weights, g_final, *, batch, seq, pos0):
    g_mix, w_in, w_pool, s_pool, w_a, w_b, w_out, g_ffn, w_gate_up, w_down = weights
    n = batch * seq
    tm = min(n, ROW_BLOCK)
    u, q, k, v, gates = _in_proj(x, g_mix, w_in, tm=tm)
    o_a = _pool(u, hist, w_pool, s_pool, batch=batch, seq=seq, tm=min(seq, POOL_ROWS), pos0=pos0)
    if past_kv is None:
        o_b = _attn_prompt(q, k, v, batch=batch, seq=seq)
    else:
        o_b = _attn_decode(q, k, v, past_kv[0], past_kv[1], batch=batch, rows=seq, past=pos0)
    h = _mix_out(o_a, o_b, gates, x, w_a, w_b, w_out, tm=min(n, MIX_ROWS))
    y = _ffn(h, g_ffn, w_gate_up, w_down, g_final, tm=tm)
    return y, u, k, v


def kernel(x_prompt, x_sample, cache_k, cache_v, state_pool, g_mix, w_in, w_pool, s_pool,
           w_branch, w_out, g_ffn, w_gate_up, w_down, g_final):
    depth = w_in.shape[0]
    assert depth == 1
    b_p, t_p, _ = x_prompt.shape
    b_s, t_s, _ = x_sample.shape
    past = cache_k.shape[2]
    l = 0
    weights = (
        g_mix[l][None, :], w_in[l].astype(BF16), w_pool[l].astype(BF16), s_pool[l][None, :],
        w_branch[l, :POOL_WIDTH].astype(BF16), w_branch[l, POOL_WIDTH:].astype(BF16),
        w_out[l].astype(BF16), g_ffn[l][None, :], w_gate_up[l].astype(BF16),
        w_down[l].astype(BF16),
    )
    g_fin = g_final[None, :]

    hist_p = jnp.zeros((b_p, HIST_ROWS, POOL_WIDTH), F32)
    y_p, u_p, k_p, v_p = _layer(
        x_prompt.reshape(b_p * t_p, D_MODEL), hist_p, None, weights, g_fin,
        batch=b_p, seq=t_p, pos0=0)

    hist_s = jnp.pad(state_pool[l], ((0, 0), (HIST_ROWS - POOL_HIST, 0), (0, 0)))
    past_kv = (cache_k[l].reshape(b_s * past * N_HEADS, HEAD_DIM),
               cache_v[l].reshape(b_s * past * N_HEADS, HEAD_DIM))
    y_s, u_s, k_s, v_s = _layer(
        x_sample.reshape(b_s * t_s, D_MODEL), hist_s, past_kv, weights, g_fin,
        batch=b_s, seq=t_s, pos0=past)

    new_pool_p = u_p.reshape(b_p, t_p, POOL_WIDTH)[:, t_p - POOL_HIST:]
    new_pool_s = jnp.concatenate([state_pool[l], u_s.reshape(b_s, t_s, POOL_WIDTH)],
                                 axis=1)[:, -POOL_HIST:]
    return (
        y_p.reshape(b_p, t_p, D_MODEL),
        y_s.reshape(b_s, t_s, D_MODEL),
        k_p.reshape(1, b_p, t_p, N_HEADS, HEAD_DIM),
        v_p.reshape(1, b_p, t_p, N_HEADS, HEAD_DIM),
        new_pool_p[None],
        k_s.reshape(1, b_s, t_s, N_HEADS, HEAD_DIM),
        v_s.reshape(1, b_s, t_s, N_HEADS, HEAD_DIM),
        new_pool_s[None],
    )
```

```python
import functools

import jax
import jax.numpy as jnp
from jax import lax
from jax.experimental import pallas as pl
from jax.experimental.pallas import tpu as pltpu

F32 = jnp.float32
BF16 = jnp.bfloat16

D_MODEL = 2048
N_HEADS = 8
HEAD_DIM = 128
ATTN_WIDTH = N_HEADS * HEAD_DIM
POOL_WINDOWS = (2, 4, 8, 16)
POOL_WIDTH = D_MODEL // 2
POOL_GROUP_WIDTH = POOL_WIDTH // len(POOL_WINDOWS)
POOL_HIST = max(POOL_WINDOWS) - 1
HIST_ROWS = POOL_HIST + 1
GROUP_WIDTH = 1024
N_GATE_GROUPS = 2 * D_MODEL // GROUP_WIDTH
D_FF = 5632
EPS = 1e-6

LANES = 128
CHUNK = 2 * LANES
Q_SCALE = 1.4426950408889634 * HEAD_DIM ** -0.5
VMEM_LIMIT = 56 * 1024 * 1024
FFN_VMEM_LIMIT = 60 * 1024 * 1024

ROW_BLOCK = 1024
WEIGHT_TILE = 512
Q_TILE = 512
DECODE_KEYS = 2048
MIX_ROWS = 512
MIX_COLS = 512
POOL_ROWS = 1024
NORM_ROWS = 256


def _params(*sem, vmem_limit=VMEM_LIMIT):
    return pltpu.CompilerParams(dimension_semantics=sem, vmem_limit_bytes=vmem_limit)


def _rmsnorm(x, g):
    ms = jnp.mean(x * x, axis=-1, keepdims=True)
    return x * lax.rsqrt(ms + EPS) * g


def _sigmoid(x):
    return 1.0 / (1.0 + jnp.exp(-x))


def _inproj_kernel(x_ref, g_ref, w_ref, u_ref, q_ref, k_ref, v_ref, gate_ref, xn_ref, *, tpg):
    j = pl.program_id(1)

    @pl.when(j == 0)
    def _():
        xn_ref[...] = _rmsnorm(x_ref[...], g_ref[...]).astype(BF16)

    def proj():
        return jnp.dot(xn_ref[...], w_ref[...], preferred_element_type=F32)

    @pl.when(j < tpg)
    def _():
        u_ref[...] = proj()

    @pl.when((j >= tpg) & (j < 2 * tpg))
    def _():
        q_ref[...] = (proj() * Q_SCALE).astype(BF16)

    @pl.when((j >= 2 * tpg) & (j < 3 * tpg))
    def _():
        k_ref[...] = proj()

    @pl.when((j >= 3 * tpg) & (j < 4 * tpg))
    def _():
        v_ref[...] = proj()

    @pl.when(j >= 4 * tpg)
    def _():
        gate_ref[...] = proj()


def _in_proj(x, g_mix, w_in, *, tm):
    n = x.shape[0]
    tn = WEIGHT_TILE
    tpg = GROUP_WIDTH // tn
    n_tiles = w_in.shape[1] // tn
    n_blocks = n // tm

    def group_spec(first, ntiles):
        def index(i, j):
            moved_on = (j >= first + ntiles) & (i + 1 < n_blocks)
            return (jnp.where(moved_on, i + 1, i),
                    jnp.where(moved_on, 0, jnp.clip(j - first, 0, ntiles - 1)))

        return pl.BlockSpec((tm, tn), index)

    def group_shape(dtype):
        return jax.ShapeDtypeStruct((n, GROUP_WIDTH), dtype)

    return pl.pallas_call(
        functools.partial(_inproj_kernel, tpg=tpg),
        grid=(n_blocks, n_tiles),
        in_specs=[
            pl.BlockSpec((tm, D_MODEL),
                         lambda i, j: (jnp.where(j > 0, jnp.minimum(i + 1, n_blocks - 1), i), 0)),
            pl.BlockSpec((1, D_MODEL), lambda i, j: (0, 0)),
            pl.BlockSpec((D_MODEL, tn), lambda i, j: (0, j)),
        ],
        out_specs=[group_spec(0, tpg), group_spec(tpg, tpg), group_spec(2 * tpg, tpg),
                   group_spec(3 * tpg, tpg), group_spec(4 * tpg, N_GATE_GROUPS * tpg)],
        out_shape=[group_shape(F32), group_shape(BF16), group_shape(F32), group_shape(F32),
                   jax.ShapeDtypeStruct((n, N_GATE_GROUPS * GROUP_WIDTH), F32)],
        scratch_shapes=[pltpu.VMEM((tm, D_MODEL), BF16)],
        compiler_params=_params("arbitrary", "arbitrary"),
        name="in_proj",
    )(x, g_mix, w_in)


def _pool_kernel(u_ref, uprev_ref, hist_ref, wp_ref, sp_ref, o_ref, buf_ref, *, tm, pos0):
    i = pl.program_id(1)
    buf_ref[HIST_ROWS:, :] = u_ref[...]

    @pl.when(i == 0)
    def _():
        buf_ref[:HIST_ROWS, :] = hist_ref[0]

    @pl.when(i > 0)
    def _():
        buf_ref[:HIST_ROWS, :] = uprev_ref[...]

    pos = pos0 + i * tm + lax.broadcasted_iota(jnp.int32, (tm, 1), 0)
    for g, w in enumerate(POOL_WINDOWS):
        cols = slice(g * POOL_GROUP_WIDTH, (g + 1) * POOL_GROUP_WIDTH)
        x = buf_ref[:, cols]
        s, m = x, 1
        while m < w:
            s = s + pltpu.roll(s, m, axis=0)
            m *= 2
        cur = x[HIST_ROWS:]
        cnt = jnp.minimum(pos + 1, w).astype(F32)
        diff = s[HIST_ROWS:] / cnt - cur
        o = jnp.dot(diff.astype(BF16), wp_ref[g], preferred_element_type=F32)
        o_ref[:, cols] = (o * sp_ref[:, cols]).astype(BF16)


def _pool(u, hist, w_pool, s_pool, *, batch, seq, tm, pos0):
    assert all(w & (w - 1) == 0 for w in POOL_WINDOWS)
    nt = seq // tm
    per = tm // HIST_ROWS
    return pl.pallas_call(
        functools.partial(_pool_kernel, tm=tm, pos0=pos0),
        grid=(batch, nt),
        in_specs=[
            pl.BlockSpec((tm, POOL_WIDTH), lambda b, i: (b * nt + i, 0)),
            pl.BlockSpec((HIST_ROWS, POOL_WIDTH),
                         lambda b, i: (jnp.maximum((b * nt + i) * per - 1, 0), 0)),
            pl.BlockSpec((1, HIST_ROWS, POOL_WIDTH), lambda b, i: (b, 0, 0)),
            pl.BlockSpec(w_pool.shape, lambda b, i: (0, 0, 0)),
            pl.BlockSpec((1, POOL_WIDTH), lambda b, i: (0, 0)),
        ],
        out_specs=pl.BlockSpec((tm, POOL_WIDTH), lambda b, i: (b * nt + i, 0)),
        out_shape=jax.ShapeDtypeStruct((batch * seq, POOL_WIDTH), BF16),
        scratch_shapes=[pltpu.VMEM((HIST_ROWS + tm, POOL_WIDTH), F32)],
        compiler_params=_params("arbitrary", "arbitrary"),
        name="pool",
    )(u, u, hist, w_pool, s_pool)


def _suffix_matrix():
    r = lax.broadcasted_iota(jnp.int32, (CHUNK, CHUNK), 0)
    c = lax.broadcasted_iota(jnp.int32, (CHUNK, CHUNK), 1)
    return jnp.where(r >= c, 1.0, 0.0).astype(BF16)


def _scores(q, kblk):
    return lax.dot_general(q, kblk, (((1,), (1,)), ((), ())), preferred_element_type=F32)


def _sb_weights(z, carry, sfx, *, stack_chunks):
    rows, span = z.shape
    n_chunks = span // CHUNK
    sp = (jnp.maximum(z, 0.0) + jnp.log2(1.0 + jnp.exp2(-jnp.abs(z)))).astype(BF16)

    def chunk(x, c):
        return x[:, c * CHUNK:(c + 1) * CHUNK]

    if stack_chunks:
        r_all = jnp.dot(jnp.concatenate([chunk(sp, c) for c in range(n_chunks)], axis=0), sfx,
                        preferred_element_type=F32)
    ws = []
    for c in reversed(range(n_chunks)):
        if stack_chunks:
            r = r_all[c * rows:(c + 1) * rows]
        else:
            r = jnp.dot(chunk(sp, c), sfx, preferred_element_type=F32)
        w = jnp.exp2(chunk(z, c) - (r + jnp.concatenate([carry] * (CHUNK // LANES), axis=1)))
        ws.insert(0, w.astype(BF16))
        carry = carry + jnp.broadcast_to(r[:, :1], (rows, LANES))
    return carry, jnp.concatenate(ws, axis=1)


def _causal_bias(rows, span):
    r = lax.broadcasted_iota(jnp.int32, (rows, span), 0)
    c = lax.broadcasted_iota(jnp.int32, (rows, span), 1)
    return jnp.where(c < r, 0.0, MASKED_LOGIT)


ATTN_MAX_UNROLL = 14
MASKED_LOGIT = -1e9


def _attn_prompt_kernel(q_ref, k_ref, v_ref, o_ref, kb_ref, vt_ref, sfx_ref, bias_ref,
                        z_ref, w_ref, carry_ref, acct_ref, *, seq):
    tile = Q_TILE
    nq = seq // tile
    kb_ref[...] = k_ref[...].astype(BF16)
    for j in range(nq):
        vt_ref[j] = v_ref[j * tile:(j + 1) * tile, :].T.astype(BF16)

    @pl.when((pl.program_id(0) == 0) & (pl.program_id(1) == 0))
    def _():
        sfx_ref[...] = _suffix_matrix()
        bias_ref[...] = _causal_bias(tile, tile)

    def rows(ref, i):
        return ref[pl.ds(pl.multiple_of(i * tile, tile), tile), :]

    def logits(t, slot):
        qi, j = jnp.minimum(t[0], nq - 1), jnp.minimum(t[1], nq - 1)
        z_ref[slot] = _scores(rows(q_ref, qi), rows(kb_ref, j))

    def weights(t, slot, diagonal):
        qi, _ = t
        if not diagonal:
            carry, w = _sb_weights(z_ref[slot], carry_ref[qi], sfx_ref[...], stack_chunks=False)
            carry_ref[qi] = carry
            w_ref[slot] = w
            return
        half = tile // 2
        for r, span in ((slice(0, half), half), (slice(half, tile), tile)):
            carry, w = _sb_weights(z_ref[slot, r, :span] + bias_ref[r, :span],
                                   jnp.zeros((half, LANES), F32), sfx_ref[...], stack_chunks=False)
            carry_ref[qi, r, :] = carry
            w_ref[slot, r, :span] = w
        w_ref[slot, :half, half:] = jnp.zeros((half, half), BF16)

    def value_product(t, slot, store):
        qi, j = t
        pv = lax.dot_general(vt_ref[j], w_ref[slot], (((1,), (1,)), ((), ())),
                             preferred_element_type=F32)
        acct_ref[qi] = pv if store else acct_ref[qi] + pv

    def walk(n_steps, following, diagonal, st):
        unroll = max(u for u in range(2, ATTN_MAX_UNROLL + 1, 2) if n_steps % u == 0)

        def group(i, st):
            prev, cur = (st[0], st[1]), (st[2], st[3])
            for s in range(unroll):
                slot = s % 2
                nxt = following(cur)
                logits(nxt, 1 - slot)
                weights(cur, slot, diagonal)
                value_product(prev, 1 - slot, store=diagonal)
                prev, cur = cur, nxt
            return prev + cur

        if n_steps == unroll:
            return group(0, st)
        return lax.fori_loop(0, n_steps // unroll, group, st)

    def next_diagonal(t):
        last = t[0] == nq - 1
        return jnp.where(last, 1, t[0] + 1), jnp.where(last, 0, t[1] + 1)

    def next_below(t):
        qi, j = t
        wrap = j == 0
        return jnp.where(wrap, qi + 1, qi), jnp.where(wrap, qi, j - 1)

    zero = jnp.int32(0)
    acct_ref[nq - 1] = jnp.zeros((HEAD_DIM, tile), F32)
    w_ref[1] = jnp.zeros((tile, tile), BF16)
    logits((zero, zero), 0)
    st = walk(nq, next_diagonal, True, (zero, zero, zero, zero))
    st = walk(nq * (nq - 1) // 2, next_below, False, st)
    value_product((st[0], st[1]), 1, store=False)
    for qi in range(nq):
        o_ref[qi * tile:(qi + 1) * tile, :] = acct_ref[qi].T.astype(BF16)


def _attn_prompt(q, k, v, *, batch, seq):
    assert seq % Q_TILE == 0
    spec = pl.BlockSpec((seq, HEAD_DIM), lambda b, h: (b, h))
    return pl.pallas_call(
        functools.partial(_attn_prompt_kernel, seq=seq),
        grid=(batch, N_HEADS),
        in_specs=[spec, spec, spec],
        out_specs=spec,
        out_shape=jax.ShapeDtypeStruct((batch * seq, ATTN_WIDTH), BF16),
        scratch_shapes=[
            pltpu.VMEM((seq, HEAD_DIM), BF16),
            pltpu.VMEM((seq // Q_TILE, HEAD_DIM, Q_TILE), BF16),
            pltpu.VMEM((CHUNK, CHUNK), BF16),
            pltpu.VMEM((Q_TILE, Q_TILE), F32),
            pltpu.VMEM((2, Q_TILE, Q_TILE), F32),
            pltpu.VMEM((2, Q_TILE, Q_TILE), BF16),
            pltpu.VMEM((seq // Q_TILE, Q_TILE, LANES), F32),
            pltpu.VMEM((seq // Q_TILE, HEAD_DIM, Q_TILE), F32),
        ],
        compiler_params=_params("arbitrary", "arbitrary"),
        name="attn_prompt",
    )(q, k, v)


def _attn_decode_kernel(q_ref, kn_ref, vn_ref, kp_ref, vp_ref, o_ref, sfx_ref, carry_ref, acc_ref,
                        *, rows):
    c = pl.program_id(1)

    def head_cols(h):
        return slice(h * HEAD_DIM, (h + 1) * HEAD_DIM)

    @pl.when(c == 0)
    def _():
        sfx_ref[...] = _suffix_matrix()
        bias = _causal_bias(rows, CHUNK)
        pad = jnp.zeros((CHUNK - rows, HEAD_DIM), BF16)
        for h in range(N_HEADS):
            q = q_ref[:, head_cols(h)]
            kn = jnp.concatenate([kn_ref[:, head_cols(h)].astype(BF16), pad], axis=0)
            vn = jnp.concatenate([vn_ref[:, head_cols(h)].astype(BF16), pad], axis=0)
            carry, w = _sb_weights(_scores(q, kn) + bias, jnp.zeros((rows, LANES), F32),
                                   sfx_ref[...], stack_chunks=True)
            carry_ref[h] = carry
            acc_ref[h] = jnp.dot(w, vn, preferred_element_type=F32)

    for h in range(N_HEADS):
        q = q_ref[:, head_cols(h)]
        k = kp_ref[pl.ds(h, DECODE_KEYS, stride=N_HEADS), :].astype(BF16)
        v = vp_ref[pl.ds(h, DECODE_KEYS, stride=N_HEADS), :].astype(BF16)
        carry, w = _sb_weights(_scores(q, k), carry_ref[h], sfx_ref[...], stack_chunks=True)
        carry_ref[h] = carry
        acc_ref[h] += jnp.dot(w, v, preferred_element_type=F32)

    @pl.when(c == pl.num_programs(1) - 1)
    def _():
        for h in range(N_HEADS):
            o_ref[:, head_cols(h)] = acc_ref[h].astype(BF16)


def _attn_decode(q, k_new, v_new, k_past, v_past, *, batch, rows, past):
    assert rows <= LANES and past % DECODE_KEYS == 0
    n_chunks = past // DECODE_KEYS
    new_spec = pl.BlockSpec((rows, ATTN_WIDTH), lambda b, c: (b, 0))
    past_spec = pl.BlockSpec((DECODE_KEYS * N_HEADS, HEAD_DIM),
                             lambda b, c: (b * n_chunks + n_chunks - 1 - c, 0))
    return pl.pallas_call(
        functools.partial(_attn_decode_kernel, rows=rows),
        grid=(batch, n_chunks),
        in_specs=[new_spec, new_spec, new_spec, past_spec, past_spec],
        out_specs=new_spec,
        out_shape=jax.ShapeDtypeStruct((batch * rows, ATTN_WIDTH), BF16),
        scratch_shapes=[pltpu.VMEM((CHUNK, CHUNK), BF16),
                        pltpu.VMEM((N_HEADS, rows, LANES), F32),
                        pltpu.VMEM((N_HEADS, rows, HEAD_DIM), F32)],
        compiler_params=_params("arbitrary", "arbitrary"),
        name="attn_decode",
    )(q, k_new, v_new, k_past, v_past)


def _mix_out_kernel(oa_ref, ob_ref, gates_ref, x_ref, wa_ref, wb_ref, wo_ref, h_ref, m_ref):
    chunks = [slice(c * MIX_COLS, (c + 1) * MIX_COLS) for c in range(D_MODEL // MIX_COLS)]
    for cols in chunks:
        ya = jnp.dot(oa_ref[...], wa_ref[:, cols], preferred_element_type=F32)
        yb = jnp.dot(ob_ref[...], wb_ref[:, cols], preferred_element_type=F32)
        gate_b_cols = slice(D_MODEL + cols.start, D_MODEL + cols.stop)
        m_ref[:, cols] = (_sigmoid(gates_ref[:, cols]) * ya
                          + _sigmoid(gates_ref[:, gate_b_cols]) * yb).astype(BF16)
    for cols in chunks:
        h_ref[:, cols] = x_ref[:, cols] + jnp.dot(m_ref[...], wo_ref[:, cols],
                                                  preferred_element_type=F32)


def _mix_out(o_a, o_b, gates, x, w_a, w_b, w_out, *, tm):
    n = x.shape[0]

    def rows(width):
        return pl.BlockSpec((tm, width), lambda i: (i, 0))

    def resident(w):
        return pl.BlockSpec(w.shape, lambda i: (0, 0), pipeline_mode=pl.Buffered(1))

    return pl.pallas_call(
        _mix_out_kernel,
        grid=(n // tm,),
        in_specs=[rows(POOL_WIDTH), rows(ATTN_WIDTH), rows(2 * D_MODEL), rows(D_MODEL),
                  resident(w_a), resident(w_b), resident(w_out)],
        out_specs=rows(D_MODEL),
        out_shape=jax.ShapeDtypeStruct((n, D_MODEL), F32),
        scratch_shapes=[pltpu.VMEM((tm, D_MODEL), BF16)],
        compiler_params=_params("arbitrary"),
        name="mix_out",
    )(o_a, o_b, gates, x, w_a, w_b, w_out)


def _ffn_kernel(h_ref, g_ref, wg_ref, wu_ref, wd_ref, gf_ref, y_ref, n_ref):
    j = pl.program_id(1)
    tm = h_ref.shape[0]
    row_chunks = [slice(r, min(r + NORM_ROWS, tm)) for r in range(0, tm, NORM_ROWS)]

    @pl.when(j == 0)
    def _():
        for rows in row_chunks:
            h = h_ref[rows, :]
            n_ref[rows, :] = _rmsnorm(h, g_ref[...]).astype(BF16)
            y_ref[rows, :] = h

    gate = jnp.dot(n_ref[...], wg_ref[...], preferred_element_type=F32)
    up = jnp.dot(n_ref[...], wu_ref[...], preferred_element_type=F32)
    hid = (gate * _sigmoid(gate) * up).astype(BF16)
    y_ref[...] += jnp.dot(hid, wd_ref[...], preferred_element_type=F32)

    @pl.when(j == pl.num_programs(1) - 1)
    def _():
        for rows in row_chunks:
            y_ref[rows, :] = _rmsnorm(y_ref[rows, :], gf_ref[...])


def _ffn(h, g_ffn, w_gate_up, w_down, g_final, *, tm):
    n = h.shape[0]
    tn = WEIGHT_TILE
    nj = D_FF // tn
    return pl.pallas_call(
        _ffn_kernel,
        grid=(n // tm, nj),
        in_specs=[
            pl.BlockSpec((tm, D_MODEL), lambda i, j: (i, 0)),
            pl.BlockSpec((1, D_MODEL), lambda i, j: (0, 0)),
            pl.BlockSpec((D_MODEL, tn), lambda i, j: (0, j)),
            pl.BlockSpec((D_MODEL, tn), lambda i, j: (0, nj + j)),
            pl.BlockSpec((tn, D_MODEL), lambda i, j: (j, 0)),
            pl.BlockSpec((1, D_MODEL), lambda i, j: (0, 0)),
        ],
        out_specs=pl.BlockSpec((tm, D_MODEL), lambda i, j: (i, 0)),
        out_shape=jax.ShapeDtypeStruct((n, D_MODEL), F32),
        scratch_shapes=[pltpu.VMEM((tm, D_MODEL), BF16)],
        compiler_params=_params("arbitrary", "arbitrary", vmem_limit=FFN_VMEM_LIMIT),
        name="ffn",
    )(h, g_ffn, w_gate_up, w_gate_up, w_down, g_final)


def _layer(x, hist, past_kv, weights, g_final, *, batch, seq, pos0):
    g_mix, w_in, w_pool, s_pool, w_a, w_b, w_out, g_ffn, w_gate_up, w_down = weights
    n = batch * seq
    tm = min(n, ROW_BLOCK)
    u, q, k, v, gates = _in_proj(x, g_mix, w_in, tm=tm)
    o_a = _pool(u, hist, w_pool, s_pool, batch=batch, seq=seq, tm=min(seq, POOL_ROWS), pos0=pos0)
    if past_kv is None:
        o_b = _attn_prompt(q, k, v, batch=batch, seq=seq)
    else:
        o_b = _attn_decode(q, k, v, past_kv[0], past_kv[1], batch=batch, rows=seq, past=pos0)
    h = _mix_out(o_a, o_b, gates, x, w_a, w_b, w_out, tm=min(n, MIX_ROWS))
    y = _ffn(h, g_ffn, w_gate_up, w_down, g_final, tm=tm)
    return y, u, k, v


def kernel(x_prompt, x_sample, cache_k, cache_v, state_pool, g_mix, w_in, w_pool, s_pool,
           w_branch, w_out, g_ffn, w_gate_up, w_down, g_final):
    depth = w_in.shape[0]
    assert depth == 1
    b_p, t_p, _ = x_prompt.shape
    b_s, t_s, _ = x_sample.shape
    past = cache_k.shape[2]
    l = 0
    weights = (
        g_mix[l][None, :], w_in[l].astype(BF16), w_pool[l].astype(BF16), s_pool[l][None, :],
        w_branch[l, :POOL_WIDTH].astype(BF16), w_branch[l, POOL_WIDTH:].astype(BF16),
        w_out[l].astype(BF16), g_ffn[l][None, :], w_gate_up[l].astype(BF16),
        w_down[l].astype(BF16),
    )
    g_fin = g_final[None, :]

    hist_p = jnp.zeros((b_p, HIST_ROWS, POOL_WIDTH), F32)
    y_p, u_p, k_p, v_p = _layer(
        x_prompt.reshape(b_p * t_p, D_MODEL), hist_p, None, weights, g_fin,
        batch=b_p, seq=t_p, pos0=0)

    hist_s = jnp.pad(state_pool[l], ((0, 0), (HIST_ROWS - POOL_HIST, 0), (0, 0)))
    past_kv = (cache_k[l].reshape(b_s * past * N_HEADS, HEAD_DIM),
               cache_v[l].reshape(b_s * past * N_HEADS, HEAD_DIM))
    y_s, u_s, k_s, v_s = _layer(
        x_sample.reshape(b_s * t_s, D_MODEL), hist_s, past_kv, weights, g_fin,
        batch=b_s, seq=t_s, pos0=past)

    new_pool_p = u_p.reshape(b_p, t_p, POOL_WIDTH)[:, t_p - POOL_HIST:]
    new_pool_s = jnp.concatenate([state_pool[l], u_s.reshape(b_s, t_s, POOL_WIDTH)],
                                 axis=1)[:, -POOL_HIST:]
    return (
        y_p.reshape(b_p, t_p, D_MODEL),
        y_s.reshape(b_s, t_s, D_MODEL),
        k_p.reshape(1, b_p, t_p, N_HEADS, HEAD_DIM),
        v_p.reshape(1, b_p, t_p, N_HEADS, HEAD_DIM),
        new_pool_p[None],
        k_s.reshape(1, b_s, t_s, N_HEADS, HEAD_DIM),
        v_s.reshape(1, b_s, t_s, N_HEADS, HEAD_DIM),
        new_pool_s[None],
    )
```

```python
import functools

import jax
import jax.numpy as jnp
from jax import lax
from jax.experimental import pallas as pl
from jax.experimental.pallas import tpu as pltpu

F32 = jnp.float32
BF16 = jnp.bfloat16

D_MODEL = 2048
N_HEADS = 8
HEAD_DIM = 128
ATTN_WIDTH = N_HEADS * HEAD_DIM
POOL_WINDOWS = (2, 4, 8, 16)
POOL_WIDTH = D_MODEL // 2
POOL_GROUP_WIDTH = POOL_WIDTH // len(POOL_WINDOWS)
POOL_HIST = max(POOL_WINDOWS) - 1
HIST_ROWS = POOL_HIST + 1
GROUP_WIDTH = 1024
N_GATE_GROUPS = 2 * D_MODEL // GROUP_WIDTH
D_FF = 5632
EPS = 1e-6

LANES = 128
CHUNK = 2 * LANES
Q_SCALE = 1.4426950408889634 * HEAD_DIM ** -0.5
VMEM_LIMIT = 56 * 1024 * 1024
FFN_VMEM_LIMIT = 60 * 1024 * 1024

ROW_BLOCK = 1024
WEIGHT_TILE = 512
Q_TILE = 512
DECODE_KEYS = 2048
MIX_ROWS = 512
MIX_COLS = 512
POOL_ROWS = 1024
NORM_ROWS = 256


def _params(*sem, vmem_limit=VMEM_LIMIT):
    return pltpu.CompilerParams(dimension_semantics=sem, vmem_limit_bytes=vmem_limit)


def _rmsnorm(x, g):
    ms = jnp.mean(x * x, axis=-1, keepdims=True)
    return x * lax.rsqrt(ms + EPS) * g


def _sigmoid(x):
    return 1.0 / (1.0 + jnp.exp(-x))


def _inproj_kernel(x_ref, g_ref, w_ref, u_ref, q_ref, k_ref, v_ref, gate_ref, xn_ref, *, tpg):
    j = pl.program_id(1)

    @pl.when(j == 0)
    def _():
        xn_ref[...] = _rmsnorm(x_ref[...], g_ref[...]).astype(BF16)

    def proj():
        return jnp.dot(xn_ref[...], w_ref[...], preferred_element_type=F32)

    @pl.when(j < tpg)
    def _():
        u_ref[...] = proj()

    @pl.when((j >= tpg) & (j < 2 * tpg))
    def _():
        q_ref[...] = (proj() * Q_SCALE).astype(BF16)

    @pl.when((j >= 2 * tpg) & (j < 3 * tpg))
    def _():
        k_ref[...] = proj()

    @pl.when((j >= 3 * tpg) & (j < 4 * tpg))
    def _():
        v_ref[...] = proj()

    @pl.when(j >= 4 * tpg)
    def _():
        gate_ref[...] = proj()


def _in_proj(x, g_mix, w_in, *, tm):
    n = x.shape[0]
    tn = WEIGHT_TILE
    tpg = GROUP_WIDTH // tn
    n_tiles = w_in.shape[1] // tn
    n_blocks = n // tm

    def group_spec(first, ntiles):
        def index(i, j):
            moved_on = (j >= first + ntiles) & (i + 1 < n_blocks)
            return (jnp.where(moved_on, i + 1, i),
                    jnp.where(moved_on, 0, jnp.clip(j - first, 0, ntiles - 1)))

        return pl.BlockSpec((tm, tn), index)

    def group_shape(dtype):
        return jax.ShapeDtypeStruct((n, GROUP_WIDTH), dtype)

    return pl.pallas_call(
        functools.partial(_inproj_kernel, tpg=tpg),
        grid=(n_blocks, n_tiles),
        in_specs=[
            pl.BlockSpec((tm, D_MODEL),
                         lambda i, j: (jnp.where(j > 0, jnp.minimum(i + 1, n_blocks - 1), i), 0)),
            pl.BlockSpec((1, D_MODEL), lambda i, j: (0, 0)),
            pl.BlockSpec((D_MODEL, tn), lambda i, j: (0, j)),
        ],
        out_specs=[group_spec(0, tpg), group_spec(tpg, tpg), group_spec(2 * tpg, tpg),
                   group_spec(3 * tpg, tpg), group_spec(4 * tpg, N_GATE_GROUPS * tpg)],
        out_shape=[group_shape(F32), group_shape(BF16), group_shape(F32), group_shape(F32),
                   jax.ShapeDtypeStruct((n, N_GATE_GROUPS * GROUP_WIDTH), F32)],
        scratch_shapes=[pltpu.VMEM((tm, D_MODEL), BF16)],
        compiler_params=_params("arbitrary", "arbitrary"),
        name="in_proj",
    )(x, g_mix, w_in)


def _pool_kernel(u_ref, uprev_ref, hist_ref, wp_ref, sp_ref, o_ref, buf_ref, *, tm, pos0):
    i = pl.program_id(1)
    buf_ref[HIST_ROWS:, :] = u_ref[...]

    @pl.when(i == 0)
    def _():
        buf_ref[:HIST_ROWS, :] = hist_ref[0]

    @pl.when(i > 0)
    def _():
        buf_ref[:HIST_ROWS, :] = uprev_ref[...]

    pos = pos0 + i * tm + lax.broadcasted_iota(jnp.int32, (tm, 1), 0)
    for g, w in enumerate(POOL_WINDOWS):
        cols = slice(g * POOL_GROUP_WIDTH, (g + 1) * POOL_GROUP_WIDTH)
        x = buf_ref[:, cols]
        s, m = x, 1
        while m < w:
            s = s + pltpu.roll(s, m, axis=0)
            m *= 2
        cur = x[HIST_ROWS:]
        cnt = jnp.minimum(pos + 1, w).astype(F32)
        diff = s[HIST_ROWS:] / cnt - cur
        o = jnp.dot(diff.astype(BF16), wp_ref[g], preferred_element_type=F32)
        o_ref[:, cols] = (o * sp_ref[:, cols]).astype(BF16)


def _pool(u, hist, w_pool, s_pool, *, batch, seq, tm, pos0):
    assert all(w & (w - 1) == 0 for w in POOL_WINDOWS)
    nt = seq // tm
    per = tm // HIST_ROWS
    return pl.pallas_call(
        functools.partial(_pool_kernel, tm=tm, pos0=pos0),
        grid=(batch, nt),
        in_specs=[
            pl.BlockSpec((tm, POOL_WIDTH), lambda b, i: (b * nt + i, 0)),
            pl.BlockSpec((HIST_ROWS, POOL_WIDTH),
                         lambda b, i: (jnp.maximum((b * nt + i) * per - 1, 0), 0)),
            pl.BlockSpec((1, HIST_ROWS, POOL_WIDTH), lambda b, i: (b, 0, 0)),
            pl.BlockSpec(w_pool.shape, lambda b, i: (0, 0, 0)),
            pl.BlockSpec((1, POOL_WIDTH), lambda b, i: (0, 0)),
        ],
        out_specs=pl.BlockSpec((tm, POOL_WIDTH), lambda b, i: (b * nt + i, 0)),
        out_shape=jax.ShapeDtypeStruct((batch * seq, POOL_WIDTH), BF16),
        scratch_shapes=[pltpu.VMEM((HIST_ROWS + tm, POOL_WIDTH), F32)],
        compiler_params=_params("arbitrary", "arbitrary"),
        name="pool",
    )(u, u, hist, w_pool, s_pool)


def _suffix_matrix():
    r = lax.broadcasted_iota(jnp.int32, (CHUNK, CHUNK), 0)
    c = lax.broadcasted_iota(jnp.int32, (CHUNK, CHUNK), 1)
    return jnp.where(r >= c, 1.0, 0.0).astype(BF16)


def _scores(q, kblk):
    return lax.dot_general(q, kblk, (((1,), (1,)), ((), ())), preferred_element_type=F32)


def _sb_weights(z, carry, sfx, *, stack_chunks):
    rows, span = z.shape
    n_chunks = span // CHUNK
    sp = (jnp.maximum(z, 0.0) + jnp.log2(1.0 + jnp.exp2(-jnp.abs(z)))).astype(BF16)

    def chunk(x, c):
        return x[:, c * CHUNK:(c + 1) * CHUNK]

    if stack_chunks:
        r_all = jnp.dot(jnp.concatenate([chunk(sp, c) for c in range(n_chunks)], axis=0), sfx,
                        preferred_element_type=F32)
    ws = []
    for c in reversed(range(n_chunks)):
        if stack_chunks:
            r = r_all[c * rows:(c + 1) * rows]
        else:
            r = jnp.dot(chunk(sp, c), sfx, preferred_element_type=F32)
        w = jnp.exp2(chunk(z, c) - (r + jnp.concatenate([carry] * (CHUNK // LANES), axis=1)))
        ws.insert(0, w.astype(BF16))
        carry = carry + jnp.broadcast_to(r[:, :1], (rows, LANES))
    return carry, jnp.concatenate(ws, axis=1)


def _causal_bias(rows, span):
    r = lax.broadcasted_iota(jnp.int32, (rows, span), 0)
    c = lax.broadcasted_iota(jnp.int32, (rows, span), 1)
    return jnp.where(c < r, 0.0, MASKED_LOGIT)


ATTN_MAX_UNROLL = 28
MASKED_LOGIT = -1e9


def _attn_prompt_kernel(q_ref, k_ref, v_ref, o_ref, kb_ref, vt_ref, sfx_ref, bias_ref,
                        z_ref, w_ref, carry_ref, acct_ref, *, seq):
    tile = Q_TILE
    nq = seq // tile
    kb_ref[...] = k_ref[...].astype(BF16)
    for j in range(nq):
        vt_ref[j] = v_ref[j * tile:(j + 1) * tile, :].T.astype(BF16)

    @pl.when((pl.program_id(0) == 0) & (pl.program_id(1) == 0))
    def _():
        sfx_ref[...] = _suffix_matrix()
        bias_ref[...] = _causal_bias(tile, tile)

    def rows(ref, i):
        return ref[pl.ds(pl.multiple_of(i * tile, tile), tile), :]

    def logits(t, slot):
        qi, j = jnp.minimum(t[0], nq - 1), jnp.minimum(t[1], nq - 1)
        z_ref[slot] = _scores(rows(q_ref, qi), rows(kb_ref, j))

    def weights(t, slot, diagonal):
        qi, _ = t
        if not diagonal:
            carry, w = _sb_weights(z_ref[slot], carry_ref[qi], sfx_ref[...], stack_chunks=False)
            carry_ref[qi] = carry
            w_ref[slot] = w
            return
        half = tile // 2
        for r, span in ((slice(0, half), half), (slice(half, tile), tile)):
            carry, w = _sb_weights(z_ref[slot, r, :span] + bias_ref[r, :span],
                                   jnp.zeros((half, LANES), F32), sfx_ref[...], stack_chunks=False)
            carry_ref[qi, r, :] = carry
            w_ref[slot, r, :span] = w
        w_ref[slot, :half, half:] = jnp.zeros((half, half), BF16)

    def value_product(t, slot, store):
        qi, j = t
        pv = lax.dot_general(vt_ref[j], w_ref[slot], (((1,), (1,)), ((), ())),
                             preferred_element_type=F32)
        acct_ref[qi] = pv if store else acct_ref[qi] + pv

    def walk(n_steps, following, diagonal, st):
        unroll = max(u for u in range(2, ATTN_MAX_UNROLL + 1, 2) if n_steps % u == 0)

        def group(i, st):
            prev, cur = (st[0], st[1]), (st[2], st[3])
            for s in range(unroll):
                slot = s % 2
                nxt = following(cur)
                logits(nxt, 1 - slot)
                weights(cur, slot, diagonal)
                value_product(prev, 1 - slot, store=diagonal)
                prev, cur = cur, nxt
            return prev + cur

        if n_steps == unroll:
            return group(0, st)
        return lax.fori_loop(0, n_steps // unroll, group, st)

    def next_diagonal(t):
        last = t[0] == nq - 1
        return jnp.where(last, 1, t[0] + 1), jnp.where(last, 0, t[1] + 1)

    def next_below(t):
        qi, j = t
        wrap = j == 0
        return jnp.where(wrap, qi + 1, qi), jnp.where(wrap, qi, j - 1)

    zero = jnp.int32(0)
    acct_ref[nq - 1] = jnp.zeros((HEAD_DIM, tile), F32)
    w_ref[1] = jnp.zeros((tile, tile), BF16)
    logits((zero, zero), 0)
    st = walk(nq, next_diagonal, True, (zero, zero, zero, zero))
    st = walk(nq * (nq - 1) // 2, next_below, False, st)
    value_product((st[0], st[1]), 1, store=False)
    for qi in range(nq):
        o_ref[qi * tile:(qi + 1) * tile, :] = acct_ref[qi].T.astype(BF16)


def _attn_prompt(q, k, v, *, batch, seq):
    assert seq % Q_TILE == 0
    spec = pl.BlockSpec((seq, HEAD_DIM), lambda b, h: (b, h))
    return pl.pallas_call(
        functools.partial(_attn_prompt_kernel, seq=seq),
        grid=(batch, N_HEADS),
        in_specs=[spec, spec, spec],
        out_specs=spec,
        out_shape=jax.ShapeDtypeStruct((batch * seq, ATTN_WIDTH), BF16),
        scratch_shapes=[
            pltpu.VMEM((seq, HEAD_DIM), BF16),
            pltpu.VMEM((seq // Q_TILE, HEAD_DIM, Q_TILE), BF16),
            pltpu.VMEM((CHUNK, CHUNK), BF16),
            pltpu.VMEM((Q_TILE, Q_TILE), F32),
            pltpu.VMEM((2, Q_TILE, Q_TILE), F32),
            pltpu.VMEM((2, Q_TILE, Q_TILE), BF16),
            pltpu.VMEM((seq // Q_TILE, Q_TILE, LANES), F32),
            pltpu.VMEM((seq // Q_TILE, HEAD_DIM, Q_TILE), F32),
        ],
        compiler_params=_params("arbitrary", "arbitrary"),
        name="attn_prompt",
    )(q, k, v)


def _attn_decode_kernel(q_ref, kn_ref, vn_ref, kp_ref, vp_ref, o_ref, sfx_ref, carry_ref, acc_ref,
                        *, rows):
    c = pl.program_id(1)

    def head_cols(h):
        return slice(h * HEAD_DIM, (h + 1) * HEAD_DIM)

    @pl.when(c == 0)
    def _():
        sfx_ref[...] = _suffix_matrix()
        bias = _causal_bias(rows, CHUNK)
        pad = jnp.zeros((CHUNK - rows, HEAD_DIM), BF16)
        for h in range(N_HEADS):
            q = q_ref[:, head_cols(h)]
            kn = jnp.concatenate([kn_ref[:, head_cols(h)].astype(BF16), pad], axis=0)
            vn = jnp.concatenate([vn_ref[:, head_cols(h)].astype(BF16), pad], axis=0)
            carry, w = _sb_weights(_scores(q, kn) + bias, jnp.zeros((rows, LANES), F32),
                                   sfx_ref[...], stack_chunks=True)
            carry_ref[h] = carry
            acc_ref[h] = jnp.dot(w, vn, preferred_element_type=F32)

    for h in range(N_HEADS):
        q = q_ref[:, head_cols(h)]
        k = kp_ref[pl.ds(h, DECODE_KEYS, stride=N_HEADS), :].astype(BF16)
        v = vp_ref[pl.ds(h, DECODE_KEYS, stride=N_HEADS), :].astype(BF16)
        carry, w = _sb_weights(_scores(q, k), carry_ref[h], sfx_ref[...], stack_chunks=True)
        carry_ref[h] = carry
        acc_ref[h] += jnp.dot(w, v, preferred_element_type=F32)

    @pl.when(c == pl.num_programs(1) - 1)
    def _():
        for h in range(N_HEADS):
            o_ref[:, head_cols(h)] = acc_ref[h].astype(BF16)


def _attn_decode(q, k_new, v_new, k_past, v_past, *, batch, rows, past):
    assert rows <= LANES and past % DECODE_KEYS == 0
    n_chunks = past // DECODE_KEYS
    new_spec = pl.BlockSpec((rows, ATTN_WIDTH), lambda b, c: (b, 0))
    past_spec = pl.BlockSpec((DECODE_KEYS * N_HEADS, HEAD_DIM),
                             lambda b, c: (b * n_chunks + n_chunks - 1 - c, 0))
    return pl.pallas_call(
        functools.partial(_attn_decode_kernel, rows=rows),
        grid=(batch, n_chunks),
        in_specs=[new_spec, new_spec, new_spec, past_spec, past_spec],
        out_specs=new_spec,
        out_shape=jax.ShapeDtypeStruct((batch * rows, ATTN_WIDTH), BF16),
        scratch_shapes=[pltpu.VMEM((CHUNK, CHUNK), BF16),
                        pltpu.VMEM((N_HEADS, rows, LANES), F32),
                        pltpu.VMEM((N_HEADS, rows, HEAD_DIM), F32)],
        compiler_params=_params("arbitrary", "arbitrary"),
        name="attn_decode",
    )(q, k_new, v_new, k_past, v_past)


def _mix_out_kernel(oa_ref, ob_ref, gates_ref, x_ref, wa_ref, wb_ref, wo_ref, h_ref, m_ref):
    chunks = [slice(c * MIX_COLS, (c + 1) * MIX_COLS) for c in range(D_MODEL // MIX_COLS)]
    for cols in chunks:
        ya = jnp.dot(oa_ref[...], wa_ref[:, cols], preferred_element_type=F32)
        yb = jnp.dot(ob_ref[...], wb_ref[:, cols], preferred_element_type=F32)
        gate_b_cols = slice(D_MODEL + cols.start, D_MODEL + cols.stop)
        m_ref[:, cols] = (_sigmoid(gates_ref[:, cols]) * ya
                          + _sigmoid(gates_ref[:, gate_b_cols]) * yb).astype(BF16)
    for cols in chunks:
        h_ref[:, cols] = x_ref[:, cols] + jnp.dot(m_ref[...], wo_ref[:, cols],
                                                  preferred_element_type=F32)


def _mix_out(o_a, o_b, gates, x, w_a, w_b, w_out, *, tm):
    n = x.shape[0]

    def rows(width):
        return pl.BlockSpec((tm, width), lambda i: (i, 0))

    def resident(w):
        return pl.BlockSpec(w.shape, lambda i: (0, 0), pipeline_mode=pl.Buffered(1))

    return pl.pallas_call(
        _mix_out_kernel,
        grid=(n // tm,),
        in_specs=[rows(POOL_WIDTH), rows(ATTN_WIDTH), rows(2 * D_MODEL), rows(D_MODEL),
                  resident(w_a), resident(w_b), resident(w_out)],
        out_specs=rows(D_MODEL),
        out_shape=jax.ShapeDtypeStruct((n, D_MODEL), F32),
        scratch_shapes=[pltpu.VMEM((tm, D_MODEL), BF16)],
        compiler_params=_params("arbitrary"),
        name="mix_out",
    )(o_a, o_b, gates, x, w_a, w_b, w_out)


def _ffn_kernel(h_ref, g_ref, wg_ref, wu_ref, wd_ref, gf_ref, y_ref, n_ref):
    j = pl.program_id(1)
    tm = h_ref.shape[0]
    row_chunks = [slice(r, min(r + NORM_ROWS, tm)) for r in range(0, tm, NORM_ROWS)]

    @pl.when(j == 0)
    def _():
        for rows in row_chunks:
            h = h_ref[rows, :]
            n_ref[rows, :] = _rmsnorm(h, g_ref[...]).astype(BF16)
            y_ref[rows, :] = h

    gate = jnp.dot(n_ref[...], wg_ref[...], preferred_element_type=F32)
    up = jnp.dot(n_ref[...], wu_ref[...], preferred_element_type=F32)
    hid = (gate * _sigmoid(gate) * up).astype(BF16)
    y_ref[...] += jnp.dot(hid, wd_ref[...], preferred_element_type=F32)

    @pl.when(j == pl.num_programs(1) - 1)
    def _():
        for rows in row_chunks:
            y_ref[rows, :] = _rmsnorm(y_ref[rows, :], gf_ref[...])


def _ffn(h, g_ffn, w_gate_up, w_down, g_final, *, tm):
    n = h.shape[0]
    tn = WEIGHT_TILE
    nj = D_FF // tn
    return pl.pallas_call(
        _ffn_kernel,
        grid=(n // tm, nj),
        in_specs=[
            pl.BlockSpec((tm, D_MODEL), lambda i, j: (i, 0)),
            pl.BlockSpec((1, D_MODEL), lambda i, j: (0, 0)),
            pl.BlockSpec((D_MODEL, tn), lambda i, j: (0, j)),
            pl.BlockSpec((D_MODEL, tn), lambda i, j: (0, nj + j)),
            pl.BlockSpec((tn, D_MODEL), lambda i, j: (j, 0)),
            pl.BlockSpec((1, D_MODEL), lambda i, j: (0, 0)),
        ],
        out_specs=pl.BlockSpec((tm, D_MODEL), lambda i, j: (i, 0)),
        out_shape=jax.ShapeDtypeStruct((n, D_MODEL), F32),
        scratch_shapes=[pltpu.VMEM((tm, D_MODEL), BF16)],
        compiler_params=_params("arbitrary", "arbitrary", vmem_limit=FFN_VMEM_LIMIT),
        name="ffn",
    )(h, g_ffn, w_gate_up, w_gate_up, w_down, g_final)


def _layer(x, hist, past_kv, weights, g_final, *, batch, seq, pos0):
    g_mix, w_in, w_pool, s_pool, w_a, w_b, w_out, g_ffn, w_gate_up, w_down = weights
    n = batch * seq
    tm = min(n, ROW_BLOCK)
    u, q, k, v, gates = _in_proj(x, g_mix, w_in, tm=tm)
    o_a = _pool(u, hist, w_pool, s_pool, batch=batch, seq=seq, tm=min(seq, POOL_ROWS), pos0=pos0)
    if past_kv is None:
        o_b = _attn_prompt(q, k, v, batch=batch, seq=seq)
    else:
        o_b = _attn_decode(q, k, v, past_kv[0], past_kv[1], batch=batch, rows=seq, past=pos0)
    h = _mix_out(o_a, o_b, gates, x, w_a, w_b, w_out, tm=min(n, MIX_ROWS))
    y = _ffn(h, g_ffn, w_gate_up, w_down, g_final, tm=tm)
    return y, u, k, v


def kernel(x_prompt, x_sample, cache_k, cache_v, state_pool, g_mix, w_in, w_pool, s_pool,
           w_branch, w_out, g_ffn, w_gate_up, w_down, g_final):
    depth = w_in.shape[0]
    assert depth == 1
    b_p, t_p, _ = x_prompt.shape
    b_s, t_s, _ = x_sample.shape
    past = cache_k.shape[2]
    l = 0
    weights = (
        g_mix[l][None, :], w_in[l].astype(BF16), w_pool[l].astype(BF16), s_pool[l][None, :],
        w_branch[l, :POOL_WIDTH].astype(BF16), w_branch[l, POOL_WIDTH:].astype(BF16),
        w_out[l].astype(BF16), g_ffn[l][None, :], w_gate_up[l].astype(BF16),
        w_down[l].astype(BF16),
    )
    g_fin = g_final[None, :]

    hist_p = jnp.zeros((b_p, HIST_ROWS, POOL_WIDTH), F32)
    y_p, u_p, k_p, v_p = _layer(
        x_prompt.reshape(b_p * t_p, D_MODEL), hist_p, None, weights, g_fin,
        batch=b_p, seq=t_p, pos0=0)

    hist_s = jnp.pad(state_pool[l], ((0, 0), (HIST_ROWS - POOL_HIST, 0), (0, 0)))
    past_kv = (cache_k[l].reshape(b_s * past * N_HEADS, HEAD_DIM),
               cache_v[l].reshape(b_s * past * N_HEADS, HEAD_DIM))
    y_s, u_s, k_s, v_s = _layer(
        x_sample.reshape(b_s * t_s, D_MODEL), hist_s, past_kv, weights, g_fin,
        batch=b_s, seq=t_s, pos0=past)

    new_pool_p = u_p.reshape(b_p, t_p, POOL_WIDTH)[:, t_p - POOL_HIST:]
    new_pool_s = jnp.concatenate([state_pool[l], u_s.reshape(b_s, t_s, POOL_WIDTH)],
                                 axis=1)[:, -POOL_HIST:]
    return (
        y_p.reshape(b_p, t_p, D_MODEL),
        y_s.reshape(b_s, t_s, D_MODEL),
        k_p.reshape(1, b_p, t_p, N_HEADS, HEAD_DIM),
        v_p.reshape(1, b_p, t_p, N_HEADS, HEAD_DIM),
        new_pool_p[None],
        k_s.reshape(1, b_s, t_s, N_HEADS, HEAD_DIM),
        v_s.reshape(1, b_s, t_s, N_HEADS, HEAD_DIM),
        new_pool_s[None],
    )
```
